```python
import math
import jax, jax.numpy as jnp
from jax import lax
import numpy as np

D_MODEL = 1024
BATCH = 4
SEQ = 4096
DEPTH = 4

GRID_W = 64
CTX_LEN = 256
D_MIX = D_MODEL
HEAD_DIM = 64
D_ATT = D_MIX // 4
D_RWKV = D_MIX // 4
D_POOL = D_MIX // 4
D_FOUR = D_MIX - D_ATT - D_RWKV - D_POOL
N_Q_HEADS = D_ATT // HEAD_DIM
N_KV_HEADS = 2
D_KV = N_KV_HEADS * HEAD_DIM
N_RWKV_HEADS = D_RWKV // HEAD_DIM
POOL_WINDOWS = (2, 4, 8, 16)
N_POOL_GROUPS = 4
POOL_GROUP = D_POOL // N_POOL_GROUPS
N_FOUR_GROUPS = 4
FOUR_GROUP = D_FOUR // N_FOUR_GROUPS
DECAY_RANK = 32
ICL_RANK = 32
GATE_RANK = 64
D_FF = 2816
Q_BLOCK = 128
ROPE_THETA = 10000.0
LN_EPS = 1e-5
QK_EPS = 1e-6
GN_EPS = 64e-5
N_SUB = 3
DEEPNORM_ALPHA = (2 * DEPTH) ** 0.25
DEEPNORM_BETA = (8 * DEPTH) ** -0.25
SPLIT_SIZES = (D_ATT, D_KV, D_KV, D_RWKV, D_RWKV, D_RWKV, D_RWKV, D_POOL, D_FOUR)
D_IN = D_ATT + 2 * D_KV + 4 * D_RWKV + D_POOL + D_FOUR

kernel_name = "hybrid_headgroup_dit_block"

F32 = jnp.float32


def _split_points():
    pts, acc = [], 0
    for s in SPLIT_SIZES[:-1]:
        acc += s
        pts.append(acc)
    return pts


def layer_norm(x, g, b):
    x32 = x.astype(F32)
    mu = jnp.mean(x32, -1, keepdims=True)
    var = jnp.mean(jnp.square(x32 - mu), -1, keepdims=True)
    return ((x32 - mu) * lax.rsqrt(var + LN_EPS) * g + b).astype(x.dtype)


def modulate(x, shift, scale):
    return x * (1 + scale) + shift


def post_norm(x, y, gate, g, b, resid_w):
    return layer_norm(DEEPNORM_ALPHA * x + resid_w * gate * y, g, b)


def swiglu(h, w_gu, w_dn):
    gate, up = jnp.split(h @ w_gu, 2, axis=-1)
    return (jax.nn.silu(gate) * up) @ w_dn


def head_rms(x, g):
    x32 = x.astype(F32)
    return (x32 * lax.rsqrt(jnp.mean(jnp.square(x32), -1, keepdims=True) + QK_EPS) * g).astype(x.dtype)


def axial_rope(T):
    n_rows = T // GRID_W
    rows = jnp.repeat(jnp.arange(n_rows), GRID_W).astype(F32)
    cols = jnp.tile(jnp.arange(GRID_W), n_rows).astype(F32)
    n_pair_axis = HEAD_DIM // 4
    inv = ROPE_THETA ** (-jnp.arange(n_pair_axis, dtype=F32) / n_pair_axis)
    ang = jnp.concatenate([rows[:, None] * inv, cols[:, None] * inv], -1)
    return jnp.cos(ang), jnp.sin(ang)


def apply_rope(x, cos, sin):
    B, T, H, dh = x.shape
    xp = x.astype(F32).reshape(B, T, H, dh // 2, 2)
    x0, x1 = xp[..., 0], xp[..., 1]
    c = cos[None, :, None, :]
    s = sin[None, :, None, :]
    out = jnp.stack([x0 * c - x1 * s, x0 * s + x1 * c], -1).reshape(B, T, H, dh)
    return out.astype(x.dtype)


def attend(q, k, v):
    B, S, Hq, dh = q.shape
    Hkv = k.shape[2]
    G = Hq // Hkv
    nb = S // Q_BLOCK
    qb = jnp.moveaxis(q.reshape(B, nb, Q_BLOCK, Hkv, G, dh), 1, 0)
    scale = dh ** -0.5

    def block(qblk):
        s = jnp.einsum('bqhgd,bkhd->bhgqk', qblk, k).astype(F32) * scale
        p = jax.nn.softmax(s, axis=-1).astype(v.dtype)
        return jnp.einsum('bhgqk,bkhd->bqhgd', p, v)

    o = lax.map(block, qb)
    return jnp.moveaxis(o, 0, 1).reshape(B, S, Hq * dh)


def attn_heads(zq, zk, zv, qg, kg):
    B, T, _ = zq.shape
    q = head_rms(zq.reshape(B, T, N_Q_HEADS, HEAD_DIM), qg)
    k = head_rms(zk.reshape(B, T, N_KV_HEADS, HEAD_DIM), kg)
    v = zv.reshape(B, T, N_KV_HEADS, HEAD_DIM)
    return q, k, v


def centred_shift(z):
    zp = jnp.pad(z, ((0, 0), (1, 1), (0, 0)))
    return 0.5 * (zp[:, :-2] + zp[:, 2:])


def rwkv_heads(t):
    B, T, _ = t.shape
    return t.reshape(B, T, N_RWKV_HEADS, HEAD_DIM)


def rwkv_prepare(z_r, z_k, z_v, z_u, p):
    mu = p['rwkv_mu']
    r = z_r + (centred_shift(z_r) - z_r) * mu[0]
    k = z_k + (centred_shift(z_k) - z_k) * mu[1]
    v = z_v + (centred_shift(z_v) - z_v) * mu[2]
    du = centred_shift(z_u) - z_u
    xw = z_u + du * mu[3]
    xa = z_u + du * mu[4]
    xg = z_u + du * mu[5]
    g = jax.nn.sigmoid(xg @ p['gate_g1']) @ p['gate_g2']
    kk = rwkv_heads(k * p['k_k']).astype(F32)
    kk = kk / jnp.maximum(jnp.sqrt(jnp.sum(jnp.square(kk), -1, keepdims=True)), 1e-12)
    dirs = []
    for d in range(2):
        w_raw = (p['decay_w0'][d] + jnp.tanh(xw @ p['decay_w1'][d]) @ p['decay_w2'][d]).astype(F32)
        decay = jnp.exp(-jnp.exp(-jax.nn.softplus(-w_raw) - 0.5))
        a = jax.nn.sigmoid(p['icl_a0'][d] + (xa @ p['icl_a1'][d]) @ p['icl_a2'][d])
        kd = k * (1 + (a - 1) * p['k_a'])
        dirs.append((rwkv_heads(decay), rwkv_heads(kd), rwkv_heads(a)))
    return rwkv_heads(r), rwkv_heads(v), kk, g, dirs


def wkv_scan(S0, r, w, k, v, a, b, reverse, emit):
    def step(S, inp):
        r_t, w_t, k_t, v_t, a_t, b_t = inp
        sa = jnp.einsum('bhvk,bhk->bhv', S, a_t)
        S = S * w_t[:, :, None, :] + sa[..., None] * b_t[:, :, None, :] + v_t[..., None] * k_t[:, :, None, :]
        y = jnp.einsum('bhvk,bhk->bhv', S, r_t) if emit else None
        return S, y

    xs = tuple(jnp.moveaxis(t.astype(F32), 1, 0) for t in (r, w, k, v, a, b))
    S_final, ys = lax.scan(step, S0, xs, reverse=reverse)
    return (jnp.moveaxis(ys, 0, 1) if emit else None), S_final


def rwkv_mix(feats, S0s, p, emit):
    r, v, kk, g, dirs = feats
    ys, bonus, finals = [], [], []
    for d, (w, kd, a) in enumerate(dirs):
        y, Sf = wkv_scan(S0s[d], r, w, kd, v, -kk, kk * a, reverse=(d == 1), emit=emit)
        finals.append(Sf)
        if emit:
            ys.append(y)
            bonus.append(jnp.sum(r * kd * p['r_k'], -1, keepdims=True).astype(F32) * v.astype(F32))
    if not emit:
        return None, finals
    B, T = r.shape[0], r.shape[1]
    y = ys[0] + ys[1]
    mu = jnp.mean(y, -1, keepdims=True)
    var = jnp.mean(jnp.square(y - mu), -1, keepdims=True)
    yn = ((y - mu) * lax.rsqrt(var + GN_EPS)).reshape(B, T, D_RWKV) * p['gn_g'] + p['gn_b']
    out = (yn + (bonus[0] + bonus[1]).reshape(B, T, D_RWKV)) * g
    return out.astype(g.dtype), finals


def centred_window_mean(z, win):
    B, T, C = z.shape
    cs = jnp.concatenate([jnp.zeros((B, 1, C), F32), jnp.cumsum(z.astype(F32), axis=1)], axis=1)
    t = jnp.arange(T)
    lo = jnp.clip(t - win // 2, 0, T)
    hi = jnp.clip(t + (win - win // 2), 0, T)
    cnt = (hi - lo).astype(F32)
    return (cs[:, hi] - cs[:, lo]) / cnt[None, :, None]


def pool_mix(z, p):
    B, T, _ = z.shape
    groups = jnp.split(z, N_POOL_GROUPS, axis=-1)
    pooled = jnp.stack([centred_window_mean(zg, w) - zg.astype(F32) for zg, w in zip(groups, POOL_WINDOWS)], axis=2)
    y = jnp.einsum('btgc,gcd->btgd', pooled.astype(z.dtype), p['pool_w']).reshape(B, T, D_POOL)
    return y * p['pool_scale']


def fourier_mix(z, p):
    B, T, _ = z.shape
    zg = z.astype(F32).reshape(B, T, N_FOUR_GROUPS, FOUR_GROUP)
    f = jnp.fft.fft2(zg, axes=(1, 3), norm='ortho').real.astype(z.dtype).reshape(B, T, D_FOUR)
    return f @ p['fourier_w']


def mixer(hl, hc, p, ctx_out):
    pts = _split_points()
    zl = jnp.split(hl @ p['w_in'], pts, axis=-1)
    zc = jnp.split(hc @ p['w_in'], pts, axis=-1)
    B = hl.shape[0]
    ql, kl, vl = attn_heads(zl[0], zl[1], zl[2], p['q_norm_g'], p['k_norm_g'])
    qc, kc, vc = attn_heads(zc[0], zc[1], zc[2], p['q_norm_g'], p['k_norm_g'])
    cos, sin = axial_rope(hl.shape[1])
    ql = apply_rope(ql, cos, sin)
    kl = apply_rope(kl, cos, sin)
    att_l = attend(ql, jnp.concatenate([kc, kl], 1), jnp.concatenate([vc, vl], 1))
    zero = jnp.zeros((B, N_RWKV_HEADS, HEAD_DIM, HEAD_DIM), F32)
    rw_c, finals = rwkv_mix(rwkv_prepare(zc[3], zc[4], zc[5], zc[6], p), (zero, zero), p, emit=ctx_out)
    rw_l, _ = rwkv_mix(rwkv_prepare(zl[3], zl[4], zl[5], zl[6], p), finals, p, emit=True)
    out_l = jnp.concatenate([att_l, rw_l, pool_mix(zl[7], p), fourier_mix(zl[8], p)], -1) @ p['w_out']
    if not ctx_out:
        return out_l, None
    att_c = attend(qc, kc, vc)
    out_c = jnp.concatenate([att_c, rw_c, pool_mix(zc[7], p), fourier_mix(zc[8], p)], -1) @ p['w_out']
    return out_l, out_c


def setup_inputs(seed: int = 0) -> dict:
    key = jax.random.key(seed)
    ks = jax.random.split(key, 40)
    L = DEPTH

    def nrm(k, shape, scale):
        return jax.random.normal(k, shape, F32) * scale

    return {
        'x': nrm(ks[0], (BATCH, SEQ, D_MODEL), 1.0),
        'c': nrm(ks[1], (BATCH, D_MODEL), 1.0),
        'ctx': nrm(ks[2], (BATCH, CTX_LEN, D_MODEL), 1.0),
        'c_ctx': nrm(ks[3], (D_MODEL,), 1.0),
        'w_mod': nrm(ks[4], (L, D_MODEL, N_SUB * 3 * D_MODEL), 0.5 * D_MODEL ** -0.5),
        'b_mod': nrm(ks[5], (L, N_SUB * 3 * D_MODEL), 0.01),
        'ln_g': 1.0 + nrm(ks[6], (L, N_SUB, D_MODEL), 0.01),
        'ln_b': nrm(ks[7], (L, N_SUB, D_MODEL), 0.01),
        'w_ffn_in': nrm(ks[8], (L, 2, D_MODEL, 2 * D_FF), D_MODEL ** -0.5),
        'w_ffn_out': nrm(ks[9], (L, 2, D_FF, D_MODEL), DEEPNORM_BETA * D_FF ** -0.5),
        'w_in': nrm(ks[10], (L, D_MODEL, D_IN), D_MODEL ** -0.5),
        'q_norm_g': 1.0 + nrm(ks[11], (L, HEAD_DIM), 0.01),
        'k_norm_g': 1.0 + nrm(ks[12], (L, HEAD_DIM), 0.01),
        'rwkv_mu': jax.random.uniform(ks[13], (L, 6, D_RWKV), F32),
        'decay_w0': jax.random.uniform(ks[14], (L, 2, D_RWKV), F32, -6.5, -1.0),
        'decay_w1': nrm(ks[15], (L, 2, D_RWKV, DECAY_RANK), 0.1 * D_RWKV ** -0.5),
        'decay_w2': nrm(ks[16], (L, 2, DECAY_RANK, D_RWKV), 0.1 * DECAY_RANK ** -0.5),
        'icl_a0': nrm(ks[17], (L, 2, D_RWKV), 0.1),
        'icl_a1': nrm(ks[18], (L, 2, D_RWKV, ICL_RANK), D_RWKV ** -0.5),
        'icl_a2': nrm(ks[19], (L, 2, ICL_RANK, D_RWKV), ICL_RANK ** -0.5),
        'gate_g1': nrm(ks[20], (L, D_RWKV, GATE_RANK), D_RWKV ** -0.5),
        'gate_g2': nrm(ks[21], (L, GATE_RANK, D_RWKV), GATE_RANK ** -0.5),
        'k_k': 0.85 + nrm(ks[22], (L, D_RWKV), 0.02),
        'k_a': 1.0 + nrm(ks[23], (L, D_RWKV), 0.02),
        'r_k': nrm(ks[24], (L, N_RWKV_HEADS, HEAD_DIM), 0.1),
        'gn_g': 1.0 + nrm(ks[25], (L, D_RWKV), 0.01),
        'gn_b': nrm(ks[26], (L, D_RWKV), 0.01),
        'pool_w': nrm(ks[27], (L, N_POOL_GROUPS, POOL_GROUP, POOL_GROUP), POOL_GROUP ** -0.5),
        'pool_scale': 1.0 + nrm(ks[28], (L, D_POOL), 0.1),
        'fourier_w': nrm(ks[29], (L, D_FOUR, D_FOUR), D_FOUR ** -0.5),
        'w_out': nrm(ks[30], (L, D_MIX, D_MODEL), DEEPNORM_BETA * D_MIX ** -0.5),
    }


def reference(x, c, ctx, c_ctx, w_mod, b_mod, ln_g, ln_b, w_ffn_in, w_ffn_out, w_in, q_norm_g, k_norm_g,
              rwkv_mu, decay_w0, decay_w1, decay_w2, icl_a0, icl_a1, icl_a2, gate_g1, gate_g2, k_k, k_a, r_k,
              gn_g, gn_b, pool_w, pool_scale, fourier_w, w_out):
    B = x.shape[0]
    xl = x
    xc = ctx
    sc = jax.nn.silu(c)
    scc = jax.nn.silu(c_ctx)
    for l in range(DEPTH):
        last = l == DEPTH - 1
        p = {
            'w_in': w_in[l], 'w_out': w_out[l], 'q_norm_g': q_norm_g[l], 'k_norm_g': k_norm_g[l],
            'rwkv_mu': rwkv_mu[l], 'decay_w0': decay_w0[l], 'decay_w1': decay_w1[l], 'decay_w2': decay_w2[l],
            'icl_a0': icl_a0[l], 'icl_a1': icl_a1[l], 'icl_a2': icl_a2[l], 'gate_g1': gate_g1[l],
            'gate_g2': gate_g2[l], 'k_k': k_k[l], 'k_a': k_a[l], 'r_k': r_k[l], 'gn_g': gn_g[l], 'gn_b': gn_b[l],
            'pool_w': pool_w[l], 'pool_scale': pool_scale[l], 'fourier_w': fourier_w[l],
        }
        ml = (sc @ w_mod[l] + b_mod[l]).reshape(B, N_SUB, 3, 1, D_MODEL)
        mc = (scc @ w_mod[l] + b_mod[l]).reshape(N_SUB, 3, 1, 1, D_MODEL)
        xl = post_norm(xl, swiglu(modulate(xl, ml[:, 0, 0], ml[:, 0, 1]), w_ffn_in[l, 0], w_ffn_out[l, 0]),
                       ml[:, 0, 2], ln_g[l, 0], ln_b[l, 0], 0.5)
        xc = post_norm(xc, swiglu(modulate(xc, mc[0, 0], mc[0, 1]), w_ffn_in[l, 0], w_ffn_out[l, 0]),
                       mc[0, 2], ln_g[l, 0], ln_b[l, 0], 0.5)
        yl, yc = mixer(modulate(xl, ml[:, 1, 0], ml[:, 1, 1]), modulate(xc, mc[1, 0], mc[1, 1]), p, not last)
        xl = post_norm(xl, yl, ml[:, 1, 2], ln_g[l, 1], ln_b[l, 1], 1.0)
        xl = post_norm(xl, swiglu(modulate(xl, ml[:, 2, 0], ml[:, 2, 1]), w_ffn_in[l, 1], w_ffn_out[l, 1]),
                       ml[:, 2, 2], ln_g[l, 2], ln_b[l, 2], 0.5)
        if not last:
            xc = post_norm(xc, yc, mc[1, 2], ln_g[l, 1], ln_b[l, 1], 1.0)
            xc = post_norm(xc, swiglu(modulate(xc, mc[2, 0], mc[2, 1]), w_ffn_in[l, 1], w_ffn_out[l, 1]),
                           mc[2, 2], ln_g[l, 2], ln_b[l, 2], 0.5)
    return xl
```

```python
import functools
import math

import jax
import jax.numpy as jnp
from jax import lax
from jax.experimental import pallas as pl
from jax.experimental.pallas import tpu as pltpu

F32 = jnp.float32
BF16 = jnp.bfloat16

HEAD_DIM = 64
GRID_W = 64
POOL_WINDOWS = (2, 4, 8, 16)
ROPE_THETA = 10000.0
LN_EPS = 1e-5
QK_EPS = 1e-6
GN_EPS = 64e-5
N_KV_HEADS = 2

TOKEN_TILE = 256
CHUNK = 64
HALO = 8
VMEM_LIMIT = 56 * 1024 * 1024


def _cparams(sem):
    return pltpu.CompilerParams(dimension_semantics=sem, vmem_limit_bytes=VMEM_LIMIT)


def _dot(a, b):
    return jnp.dot(a, b, preferred_element_type=F32)


def _dot_nt(a, b):
    return lax.dot_general(a, b, (((1,), (1,)), ((), ())), preferred_element_type=F32)


def _dot_tn(a, b):
    return lax.dot_general(a, b, (((0,), (0,)), ((), ())), preferred_element_type=F32)


def _hi_lo(x):
    hi = x.astype(BF16)
    lo = (x - hi.astype(F32)).astype(BF16)
    return hi, lo


def _split3(x):
    x1 = x.astype(BF16)
    r1 = x - x1.astype(F32)
    x2 = r1.astype(BF16)
    x3 = (r1 - x2.astype(F32)).astype(BF16)
    return x1, x2, x3


def _mm1(a, b, dot=_dot):
    return dot(a.astype(BF16), b.astype(BF16))


def _mm3(a, b, dot=_dot):
    a1, a2 = _hi_lo(a)
    b1, b2 = _hi_lo(b)
    return dot(a1, b1) + (dot(a1, b2) + dot(a2, b1))


def _mm_sel(sel_bf16, x):
    x1, x2, x3 = _split3(x)
    return _dot(sel_bf16, x1) + (_dot(sel_bf16, x2) + _dot(sel_bf16, x3))


def _mm_sel_r(x, sel_bf16):
    x1, x2, x3 = _split3(x)
    return _dot(x1, sel_bf16) + (_dot(x2, sel_bf16) + _dot(x3, sel_bf16))


def _group_ones(n, group):
    r = lax.broadcasted_iota(jnp.int32, (n, n), 0) // group
    c = lax.broadcasted_iota(jnp.int32, (n, n), 1) // group
    return r == c


def _layer_norm(t, g, b):
    mu = jnp.mean(t, -1, keepdims=True)
    tc = t - mu
    var = jnp.mean(tc * tc, -1, keepdims=True)
    return tc * lax.rsqrt(var + LN_EPS) * g + b


def _mod_kernel(c_ref, w_ref, b_ref, o_ref):
    c = c_ref[...]
    s = c * jax.nn.sigmoid(c)
    o_ref[0] = _mm1(s, w_ref[0]) + b_ref[0]


def _modulation(cc, w_mod, b_mod):
    depth, d, n = w_mod.shape
    tn = n // 4
    return pl.pallas_call(
        _mod_kernel,
        grid=(depth, n // tn),
        in_specs=[
            pl.BlockSpec((8, d), lambda l, j: (0, 0)),
            pl.BlockSpec((1, d, tn), lambda l, j: (l, 0, j)),
            pl.BlockSpec((1, 1, tn), lambda l, j: (l, 0, j)),
        ],
        out_specs=pl.BlockSpec((1, 8, tn), lambda l, j: (l, 0, j)),
        out_shape=jax.ShapeDtypeStruct((depth, 8, n), F32),
        compiler_params=_cparams(("arbitrary", "arbitrary")),
        name="modulation",
    )(cc, w_mod, b_mod.reshape(depth, 1, n))


def _mod_row_map(n_batch):
    return lambda b, i: (jnp.where(i == 0, n_batch, b), 0, 0)


def _ffn_kernel(x_ref, m_ref, wgu_ref, wdn_ref, g_ref, b_ref, o_ref, *, sub, dff, fc, alpha):
    x = x_ref[0]
    shift = m_ref[0, 3 * sub:3 * sub + 1, :]
    scale = m_ref[0, 3 * sub + 1:3 * sub + 2, :]
    gate = m_ref[0, 3 * sub + 2:3 * sub + 3, :]
    h = (x * (1.0 + scale) + shift).astype(BF16)
    acc = jnp.zeros(x.shape, F32)
    for j in range(dff // fc):
        gg = _dot(h, wgu_ref[:, j * fc:(j + 1) * fc])
        uu = _dot(h, wgu_ref[:, dff + j * fc:dff + (j + 1) * fc])
        act = (gg * jax.nn.sigmoid(gg) * uu).astype(BF16)
        acc = acc + _dot(act, wdn_ref[j * fc:(j + 1) * fc, :])
    t = alpha * x + (0.5 * gate) * acc
    o_ref[0] = _layer_norm(t, g_ref[...], b_ref[...])


def _ffn(xs, mod_l, wgu, wdn, ln_g, ln_b, *, sub, alpha, n_batch):
    b, t, d = xs.shape
    dff = wdn.shape[0]
    tm = TOKEN_TILE
    kern = functools.partial(_ffn_kernel, sub=sub, dff=dff, fc=256, alpha=alpha)
    return pl.pallas_call(
        kern,
        grid=(b, t // tm),
        in_specs=[
            pl.BlockSpec((1, tm, d), lambda bb, i: (bb, i, 0)),
            pl.BlockSpec((1, 9, d), _mod_row_map(n_batch)),
            pl.BlockSpec((d, 2 * dff), lambda bb, i: (0, 0), pipeline_mode=pl.Buffered(1)),
            pl.BlockSpec((dff, d), lambda bb, i: (0, 0), pipeline_mode=pl.Buffered(1)),
            pl.BlockSpec((1, d), lambda bb, i: (0, 0)),
            pl.BlockSpec((1, d), lambda bb, i: (0, 0)),
        ],
        out_specs=pl.BlockSpec((1, tm, d), lambda bb, i: (bb, i, 0)),
        out_shape=jax.ShapeDtypeStruct((b, t, d), F32),
        compiler_params=_cparams(("arbitrary", "arbitrary")),
        name=f"ffn{sub}",
    )(xs, mod_l, wgu, wdn, ln_g.reshape(1, d), ln_b.reshape(1, d))


def _inproj_kernel(x_ref, m_ref, w_ref, o_ref):
    x = x_ref[0]
    shift = m_ref[0, 3:4, :]
    scale = m_ref[0, 4:5, :]
    h = (x * (1.0 + scale) + shift).astype(BF16)
    o_ref[0] = _dot(h, w_ref[...])


def _inproj(xs, mod_l, w_in, *, n_batch):
    b, t, d = xs.shape
    n = w_in.shape[1]
    tm = TOKEN_TILE
    return pl.pallas_call(
        _inproj_kernel,
        grid=(b, t // tm),
        in_specs=[
            pl.BlockSpec((1, tm, d), lambda bb, i: (bb, i, 0)),
            pl.BlockSpec((1, 9, d), _mod_row_map(n_batch)),
            pl.BlockSpec((d, n), lambda bb, i: (0, 0), pipeline_mode=pl.Buffered(1)),
        ],
        out_specs=pl.BlockSpec((1, tm, n), lambda bb, i: (bb, i, 0)),
        out_shape=jax.ShapeDtypeStruct((b, t, n), F32),
        compiler_params=_cparams(("arbitrary", "arbitrary")),
        name="inproj",
    )(xs, mod_l, w_in)


def _pair_swap(x):
    n = x.shape[-1]
    lane = lax.broadcasted_iota(jnp.int32, x.shape, 1)
    nxt = pltpu.roll(x, n - 1, 1)
    prv = pltpu.roll(x, 1, 1)
    return jnp.where(lane % 2 == 0, nxt, prv)


def _qkv_kernel(z_ref, cos_ref, sin_ref, qg_ref, kg_ref, q_ref, k_ref, v_ref, *, n_q, n_kv):
    z = z_ref[0]
    dq, dk = n_q * HEAD_DIM, n_kv * HEAD_DIM
    zq, zk, zv = z[:, :dq], z[:, dq:dq + dk], z[:, dq + dk:dq + 2 * dk]
    cos, sin = cos_ref[...], sin_ref[...]
    xq = zq * qg_ref[...]
    yq = xq * cos + _pair_swap(xq) * sin
    xk = zk * kg_ref[...]
    yk = xk * cos[:, :dk] + _pair_swap(xk) * sin[:, :dk]
    for h in range(n_q):
        sl = slice(h * HEAD_DIM, (h + 1) * HEAD_DIM)
        ms = jnp.mean(zq[:, sl] * zq[:, sl], -1, keepdims=True)
        q_ref[0, h] = (yq[:, sl] * (lax.rsqrt(ms + QK_EPS) * HEAD_DIM ** -0.5)).astype(BF16)
    for h in range(n_kv):
        sl = slice(h * HEAD_DIM, (h + 1) * HEAD_DIM)
        ms = jnp.mean(zk[:, sl] * zk[:, sl], -1, keepdims=True)
        k_ref[0, h] = (yk[:, sl] * lax.rsqrt(ms + QK_EPS)).astype(BF16)
        v_ref[0, h] = zv[:, sl].astype(BF16)


def _attn_kernel(q_ref, k_ref, v_ref, o_ref, *, n_ctx, group):
    i = pl.program_id(2)

    def run(kk, vv):
        outs = []
        for g in range(group):
            s = _dot_nt(q_ref[0, g], kk)
            m = jnp.max(s, -1, keepdims=True)
            p = jnp.exp(s - m)
            l = jnp.sum(p, -1, keepdims=True)
            outs.append(_dot(p.astype(BF16), vv) / l)
        o_ref[0] = jnp.concatenate(outs, -1)

    @pl.when(i == 0)
    def _():
        run(k_ref[0, 0, :n_ctx], v_ref[0, 0, :n_ctx])

    @pl.when(i > 0)
    def _():
        run(k_ref[0, 0], v_ref[0, 0])


def _attention(z, cos_t, sin_t, qg, kg, *, n_ctx, att_col):
    b, t, _ = z.shape
    tm = TOKEN_TILE
    n_q = qg.shape[1] // HEAD_DIM
    n_kv = N_KV_HEADS
    group = n_q // n_kv
    wq = (n_q + 2 * n_kv) * HEAD_DIM
    q, k, v = pl.pallas_call(
        functools.partial(_qkv_kernel, n_q=n_q, n_kv=n_kv),
        grid=(b, t // tm),
        in_specs=[
            pl.BlockSpec((1, tm, wq), lambda bb, i: (bb, i, att_col // wq)),
            pl.BlockSpec((tm, n_q * HEAD_DIM), lambda bb, i: (i, 0)),
            pl.BlockSpec((tm, n_q * HEAD_DIM), lambda bb, i: (i, 0)),
            pl.BlockSpec((1, n_q * HEAD_DIM), lambda bb, i: (0, 0)),
            pl.BlockSpec((1, n_kv * HEAD_DIM), lambda bb, i: (0, 0)),
        ],
        out_specs=[
            pl.BlockSpec((1, n_q, tm, HEAD_DIM), lambda bb, i: (bb, 0, i, 0)),
            pl.BlockSpec((1, n_kv, tm, HEAD_DIM), lambda bb, i: (bb, 0, i, 0)),
            pl.BlockSpec((1, n_kv, tm, HEAD_DIM), lambda bb, i: (bb, 0, i, 0)),
        ],
        out_shape=[
            jax.ShapeDtypeStruct((b, n_q, t, HEAD_DIM), BF16),
            jax.ShapeDtypeStruct((b, n_kv, t, HEAD_DIM), BF16),
            jax.ShapeDtypeStruct((b, n_kv, t, HEAD_DIM), BF16),
        ],
        compiler_params=_cparams(("arbitrary", "arbitrary")),
        name="qkv_prep",
    )(z, cos_t, sin_t, qg, kg[:, :n_kv * HEAD_DIM])
    return pl.pallas_call(
        functools.partial(_attn_kernel, n_ctx=n_ctx, group=group),
        grid=(b, n_kv, t // tm),
        in_specs=[
            pl.BlockSpec((1, group, tm, HEAD_DIM), lambda bb, h, i: (bb, h, i, 0)),
            pl.BlockSpec((1, 1, t, HEAD_DIM), lambda bb, h, i: (bb, h, 0, 0)),
            pl.BlockSpec((1, 1, t, HEAD_DIM), lambda bb, h, i: (bb, h, 0, 0)),
        ],
        out_specs=pl.BlockSpec((1, tm, group * HEAD_DIM), lambda bb, h, i: (bb, i, h)),
        out_shape=jax.ShapeDtypeStruct((b, t, n_q * HEAD_DIM), F32),
        compiler_params=_cparams(("arbitrary", "arbitrary", "arbitrary")),
        name="attention",
    )(q, k, v)


def _halo_specs(width, col_block, rows_per_tile, n_tiles):
    r = rows_per_tile // HALO

    def prev_map(bb, i, *_):
        return (bb, jnp.maximum(i * r - 1, 0), col_block)

    def next_map(bb, i, *_):
        return (bb, jnp.minimum((i + 1) * r, n_tiles * r - 1), col_block)

    return (pl.BlockSpec((1, HALO, width), prev_map), pl.BlockSpec((1, HALO, width), next_map))


def _segment_flags(i, n_tiles):
    has_prev = i >= 2
    has_next = jnp.logical_and(i >= 1, i < n_tiles - 1)
    return has_prev, has_next


def _tri_inverse(lmat, mm):
    n = lmat.shape[0]
    eye = (lax.broadcasted_iota(jnp.int32, (n, n), 0) == lax.broadcasted_iota(jnp.int32, (n, n), 1))
    acc = jnp.where(eye, 1.0, 0.0) + lmat
    power = lmat
    for _ in range(int(math.log2(CHUNK)) - 1):
        power = mm(power, power)
        acc = acc + mm(power, acc)
    return acc


def _rwkv_chunk_kernel(z_ref, zp_ref, zn_ref, mu_ref, kk_ref, ka_ref, rk_ref, w0_ref, w1_ref, w2_ref,
                       a0_ref, a1_ref, a2_ref, g1_ref, g2_ref,
                       gm_ref, hm_ref, ry_ref, y0_ref, gate_ref, bonus_ref, *, dr):
    i = pl.program_id(1)
    d = pl.program_id(2)
    n_tiles = pl.num_programs(1)
    tm = z_ref.shape[1]
    z = z_ref[0]
    has_prev, has_next = _segment_flags(i, n_tiles)
    prow = jnp.where(has_prev, zp_ref[0, HALO - 1:HALO, :], 0.0)
    nrow = jnp.where(has_next, zn_ref[0, 0:1, :], 0.0)
    row = lax.broadcasted_iota(jnp.int32, (tm, 1), 0)
    zprev = jnp.where(row == 0, prow, pltpu.roll(z, 1, 0))
    znext = jnp.where(row == tm - 1, nrow, pltpu.roll(z, tm - 1, 0))
    dz = 0.5 * (zprev + znext) - z
    mu = mu_ref[...]
    r = z[:, 0:dr] + dz[:, 0:dr] * mu[0:1]
    k = z[:, dr:2 * dr] + dz[:, dr:2 * dr] * mu[1:2]
    v = z[:, 2 * dr:3 * dr] + dz[:, 2 * dr:3 * dr] * mu[2:3]
    zu, du = z[:, 3 * dr:4 * dr], dz[:, 3 * dr:4 * dr]
    xw = zu + du * mu[3:4]
    xa = zu + du * mu[4:5]
    xg = zu + du * mu[5:6]

    head_ones = jnp.where(_group_ones(dr, HEAD_DIM), 1.0, 0.0).astype(BF16)
    gate_ref[0] = _mm1(jax.nn.sigmoid(_mm1(xg, g1_ref[...])), g2_ref[...])
    kk = k * kk_ref[...]
    kk = kk / jnp.maximum(jnp.sqrt(_mm_sel_r(kk * kk, head_ones)), 1e-12)

    w_raw = w0_ref[pl.ds(d, 1), :] + _mm1(jnp.tanh(_mm1(xw, w1_ref[d])), w2_ref[d])
    logw = (-math.exp(-0.5)) * jax.nn.sigmoid(w_raw)
    aicl = jax.nn.sigmoid(a0_ref[pl.ds(d, 1), :] + _mm1(_mm1(xa, a1_ref[d]), a2_ref[d]))
    kd = k * (1.0 + (aicl - 1.0) * ka_ref[...])
    bvec = kk * aicl
    bonus_ref[0, 0] = _mm_sel_r(r * kd * rk_ref[...], head_ones) * v

    ri = lax.broadcasted_iota(jnp.int32, (tm, tm), 0)
    ci = lax.broadcasted_iota(jnp.int32, (tm, tm), 1)
    same_chunk = (ri // CHUNK) == (ci // CHUNK)
    before = (ci - ri) * (1 - 2 * d) < 0
    strict = jnp.logical_and(same_chunk, before)
    incl = jnp.logical_and(same_chunk, jnp.logical_or(before, ci == ri))
    cw = _mm_sel(jnp.where(incl, 1.0, 0.0).astype(BF16), logw)
    tot = _mm_sel(jnp.where(same_chunk, 1.0, 0.0).astype(BF16), logw)
    e_in = jnp.exp(cw)
    e_neg = jnp.exp(-cw)
    e_rem = jnp.exp(tot - cw)
    a_t = -kk * jnp.exp(cw - logw)
    r_t = r * e_in
    b_h = bvec * e_neg
    k_h = kd * e_neg
    b_b = bvec * e_rem
    k_b = kd * e_rem
    rhs_nt = jnp.concatenate([b_h, k_h], 0)
    lane = lax.broadcasted_iota(jnp.int32, (1, dr), 1)
    mm = _mm3

    def head_body(h, carry):
        p_nat, q_nat, ry, y0 = carry
        mh = (lane // HEAD_DIM) == h
        a_m = jnp.where(mh, a_t, 0.0)
        v_m = jnp.where(mh, v, 0.0)
        lhs = jnp.concatenate([a_m, jnp.where(mh, r_t, 0.0)], 0)
        prod = mm(lhs, rhs_nt, _dot_nt)
        lab = jnp.where(strict, prod[:tm, :tm], 0.0)
        lak = jnp.where(strict, prod[:tm, tm:], 0.0)
        mrb = jnp.where(incl, prod[tm:, :tm], 0.0)
        mrk = jnp.where(incl, prod[tm:, tm:], 0.0)
        tinv = _tri_inverse(lab, mm)
        pq = mm(tinv, jnp.concatenate([a_m, mm(lak, v_m)], 1))
        ph, qh = pq[:, :dr], pq[:, dr:]
        ry = ry + mm(mrb, ph)
        y0 = y0 + (mm(mrb, qh) + mm(mrk, v_m))
        return p_nat + ph, q_nat + qh, ry, y0

    zero = jnp.zeros((tm, dr), F32)
    p_nat, q_nat, ry, y0 = lax.fori_loop(0, dr // HEAD_DIM, head_body, (zero, zero, r_t, zero))
    ry_ref[0, 0] = ry
    y0_ref[0, 0] = y0

    same_head = _group_ones(dr, HEAD_DIM)
    eye = lax.broadcasted_iota(jnp.int32, (dr, dr), 0) == lax.broadcasted_iota(jnp.int32, (dr, dr), 1)
    for c in range(tm // CHUNK):
        sl = slice(c * CHUNK, (c + 1) * CHUNK)
        gmat = mm(b_b[sl], p_nat[sl], _dot_tn)
        hmat = mm(jnp.concatenate([b_b[sl], k_b[sl]], 0), jnp.concatenate([q_nat[sl], v[sl]], 0), _dot_tn)
        decay = jnp.exp(tot[c * CHUNK:c * CHUNK + 1, :])
        gm_ref[0, 0, c] = jnp.where(same_head, gmat, 0.0) + jnp.where(eye, decay, 0.0)
        hm_ref[0, 0, c] = jnp.where(same_head, hmat, 0.0)


def _rwkv_state_kernel(gf_ref, hf_ref, gr_ref, hr_ref, sf_ref, sr_ref, st_ref):
    s = pl.program_id(0)

    @pl.when(s == 0)
    def _():
        st_ref[...] = jnp.zeros(st_ref.shape, F32)

    nb = gf_ref.shape[0]
    for b in range(nb):
        for d, (g_ref, h_ref, s_ref) in enumerate(((gf_ref, hf_ref, sf_ref), (gr_ref, hr_ref, sr_ref))):
            st = st_ref[b, d]
            s_ref[b, 0, 0] = st
            st_ref[b, d] = jnp.dot(g_ref[b, 0, 0], st, preferred_element_type=F32,
                                   precision=lax.Precision.HIGHEST) + h_ref[b, 0, 0]


def _rwkv_out_kernel(ry_ref, y0_ref, sf_ref, sr_ref, gate_ref, bonus_ref, gg_ref, gb_ref, o_ref):
    tm, dr = o_ref.shape[1], o_ref.shape[2]
    y = y0_ref[0, 0] + y0_ref[0, 1]
    parts = []
    for c in range(tm // CHUNK):
        sl = slice(c * CHUNK, (c + 1) * CHUNK)
        parts.append(_mm3(ry_ref[0, 0, sl, :], sf_ref[0, 0, c]) + _mm3(ry_ref[0, 1, sl, :], sr_ref[0, 0, c]))
    y = y + jnp.concatenate(parts, 0)
    head_ones = jnp.where(_group_ones(dr, HEAD_DIM), 1.0, 0.0).astype(BF16)
    mu = _mm_sel_r(y, head_ones) * (1.0 / HEAD_DIM)
    yc = y - mu
    var = _mm_sel_r(yc * yc, head_ones) * (1.0 / HEAD_DIM)
    yn = yc * lax.rsqrt(var + GN_EPS) * gg_ref[...] + gb_ref[...]
    o_ref[0] = (yn + (bonus_ref[0, 0] + bonus_ref[0, 1])) * gate_ref[0]


def _rwkv(z, p, *, n_ctx):
    b, t, _ = z.shape
    dr = p["k_k"].shape[-1]
    tm = TOKEN_TILE
    nt = t // tm
    cpt = tm // CHUNK
    nc = t // CHUNK
    nc_ctx = n_ctx // CHUNK
    wz = 4 * dr
    prev_spec, next_spec = _halo_specs(wz, 0, tm, nt)

    def full(shape):
        return pl.BlockSpec(shape, lambda bb, i, d: (0,) * len(shape))

    gm, hm, ry, y0, gate, bonus = pl.pallas_call(
        functools.partial(_rwkv_chunk_kernel, dr=dr),
        grid=(b, nt, 2),
        in_specs=[
            pl.BlockSpec((1, tm, wz), lambda bb, i, d: (bb, i, 0)), prev_spec, next_spec,
            full((6, dr)), full((1, dr)), full((1, dr)), full((1, dr)),
            full((2, dr)), full(p["decay_w1"].shape), full(p["decay_w2"].shape),
            full((2, dr)), full(p["icl_a1"].shape), full(p["icl_a2"].shape),
            full(p["gate_g1"].shape), full(p["gate_g2"].shape),
        ],
        out_specs=[
            pl.BlockSpec((1, 1, cpt, dr, dr), lambda bb, i, d: (bb, d, i, 0, 0)),
            pl.BlockSpec((1, 1, cpt, dr, dr), lambda bb, i, d: (bb, d, i, 0, 0)),
            pl.BlockSpec((1, 1, tm, dr), lambda bb, i, d: (bb, d, i, 0)),
            pl.BlockSpec((1, 1, tm, dr), lambda bb, i, d: (bb, d, i, 0)),
            pl.BlockSpec((1, tm, dr), lambda bb, i, d: (bb, i, 0)),
            pl.BlockSpec((1, 1, tm, dr), lambda bb, i, d: (bb, d, i, 0)),
        ],
        out_shape=[
            jax.ShapeDtypeStruct((b, 2, nc, dr, dr), F32),
            jax.ShapeDtypeStruct((b, 2, nc, dr, dr), F32),
            jax.ShapeDtypeStruct((b, 2, t, dr), F32),
            jax.ShapeDtypeStruct((b, 2, t, dr), F32),
            jax.ShapeDtypeStruct((b, t, dr), F32),
            jax.ShapeDtypeStruct((b, 2, t, dr), F32),
        ],
        compiler_params=_cparams(("arbitrary", "arbitrary", "arbitrary")),
        name="rwkv_chunks",
    )(z, z, z, p["rwkv_mu"], p["k_k"].reshape(1, dr), p["k_a"].reshape(1, dr), p["r_k"].reshape(1, dr),
      p["decay_w0"], p["decay_w1"], p["decay_w2"], p["icl_a0"], p["icl_a1"], p["icl_a2"],
      p["gate_g1"], p["gate_g2"])

    def fwd_map(s):
        return (0, 0, s, 0, 0)

    def rev_map(s):
        return (0, 1, jnp.where(s < nc_ctx, nc_ctx - 1 - s, nc + nc_ctx - 1 - s), 0, 0)

    blk = (b, 1, 1, dr, dr)
    s_fwd, s_rev = pl.pallas_call(
        _rwkv_state_kernel,
        grid=(nc,),
        in_specs=[pl.BlockSpec(blk, fwd_map), pl.BlockSpec(blk, fwd_map),
                  pl.BlockSpec(blk, rev_map), pl.BlockSpec(blk, rev_map)],
        out_specs=[pl.BlockSpec(blk, lambda s: (0, 0, s, 0, 0)),
                   pl.BlockSpec(blk, lambda s: (0, 0, rev_map(s)[2], 0, 0))],
        out_shape=[jax.ShapeDtypeStruct((b, 1, nc, dr, dr), F32),
                   jax.ShapeDtypeStruct((b, 1, nc, dr, dr), F32)],
        scratch_shapes=[pltpu.VMEM((b, 2, dr, dr), F32)],
        compiler_params=_cparams(("arbitrary",)),
        name="rwkv_state",
    )(gm, hm, gm, hm)

    return pl.pallas_call(
        _rwkv_out_kernel,
        grid=(b, nt),
        in_specs=[
            pl.BlockSpec((1, 2, tm, dr), lambda bb, i: (bb, 0, i, 0)),
            pl.BlockSpec((1, 2, tm, dr), lambda bb, i: (bb, 0, i, 0)),
            pl.BlockSpec((1, 1, cpt, dr, dr), lambda bb, i: (bb, 0, i, 0, 0)),
            pl.BlockSpec((1, 1, cpt, dr, dr), lambda bb, i: (bb, 0, i, 0, 0)),
            pl.BlockSpec((1, tm, dr), lambda bb, i: (bb, i, 0)),
            pl.BlockSpec((1, 2, tm, dr), lambda bb, i: (bb, 0, i, 0)),
            pl.BlockSpec((1, dr), lambda bb, i: (0, 0)),
            pl.BlockSpec((1, dr), lambda bb, i: (0, 0)),
        ],
        out_specs=pl.BlockSpec((1, tm, dr), lambda bb, i: (bb, i, 0)),
        out_shape=jax.ShapeDtypeStruct((b, t, dr), F32),
        compiler_params=_cparams(("arbitrary", "arbitrary")),
        name="rwkv_out",
    )(ry, y0, s_fwd, s_rev, gate, bonus, p["gn_g"].reshape(1, dr), p["gn_b"].reshape(1, dr))


def _pool_kernel(z_ref, zp_ref, zn_ref, w_ref, sc_ref, o_ref, *, n_ctx):
    i = pl.program_id(1)
    n_tiles = pl.num_programs(1)
    tm, dp = z_ref.shape[1], z_ref.shape[2]
    has_prev, has_next = _segment_flags(i, n_tiles)
    x = z_ref[0]
    xe = jnp.concatenate([jnp.where(has_prev, zp_ref[0], 0.0), x, jnp.where(has_next, zn_ref[0], 0.0)], 0)
    ne = tm + 2 * HALO

    def back(a, s):
        return pltpu.roll(a, s, 0)

    def ahead(a, s):
        return pltpu.roll(a, ne - s, 0)

    d2 = xe + back(xe, 1)
    d4 = d2 + back(d2, 2)
    d8 = d4 + back(d4, 4)
    e2 = xe + ahead(xe, 1)
    e4 = e2 + ahead(e2, 2)
    f1 = ahead(xe, 1)
    f3 = f1 + ahead(e2, 2)
    f7 = f3 + ahead(e4, 4)
    sums = (d2, d2 + back(xe, 2) + f1, d4 + back(xe, 4) + f3, d8 + back(xe, 8) + f7)

    tglob = i * tm + lax.broadcasted_iota(jnp.int32, (tm, 1), 0)
    n_lat = n_tiles * tm - n_ctx
    pos = jnp.where(i == 0, tglob, tglob - n_ctx)
    seg = jnp.where(i == 0, n_ctx, n_lat)
    lane = lax.broadcasted_iota(jnp.int32, (1, dp), 1)
    gw = dp // len(POOL_WINDOWS)
    pooled = jnp.zeros((tm, dp), F32)
    for g, win in enumerate(POOL_WINDOWS):
        nb, nf = win // 2, win - win // 2 - 1
        cnt = (jnp.minimum(pos, nb) + 1 + jnp.minimum(seg - 1 - pos, nf)).astype(F32)
        mean = sums[g][HALO:HALO + tm] / cnt
        pooled = jnp.where((lane // gw) == g, mean - x, pooled)
    o_ref[0] = _mm1(pooled, w_ref[...]) * sc_ref[...]


def _pool(z, w_bd, scale, *, n_ctx, col):
    b, t, _ = z.shape
    dp = w_bd.shape[0]
    tm = TOKEN_TILE
    prev_spec, next_spec = _halo_specs(dp, col // dp, tm, t // tm)
    return pl.pallas_call(
        functools.partial(_pool_kernel, n_ctx=n_ctx),
        grid=(b, t // tm),
        in_specs=[
            pl.BlockSpec((1, tm, dp), lambda bb, i: (bb, i, col // dp)), prev_spec, next_spec,
            pl.BlockSpec((dp, dp), lambda bb, i: (0, 0)),
            pl.BlockSpec((1, dp), lambda bb, i: (0, 0)),
        ],
        out_specs=pl.BlockSpec((1, tm, dp), lambda bb, i: (bb, i, 0)),
        out_shape=jax.ShapeDtypeStruct((b, t, dp), F32),
        compiler_params=_cparams(("arbitrary", "arbitrary")),
        name="pool",
    )(z, z, z, w_bd, scale.reshape(1, dp))


def _chan_dft_kernel(z_ref, cs_ref, o_ref):
    o_ref[0] = _mm3(z_ref[0], cs_ref[...]).astype(BF16)


def _time_dft_kernel(ct_ref, st_ref, ab_ref, w_ref, o_ref, *, norm):
    df = w_ref.shape[0]
    ab = ab_ref[0]
    f = (_dot(ct_ref[...], ab[:, :df]) + _dot(st_ref[...], ab[:, df:])) * norm
    o_ref[0] = _mm1(f, w_ref[...])


def _dft_tables(n):
    idx = (jnp.arange(n, dtype=jnp.int32)[:, None] * jnp.arange(n, dtype=jnp.int32)[None, :]) % n
    ang = idx.astype(F32) * (2.0 * math.pi / n)
    return jnp.cos(ang), jnp.sin(ang)


def _fourier_segment(ab, tables, w, *, rows, n_chan_group):
    b = ab.shape[0]
    df = w.shape[0]
    ct, st = tables
    tr = min(rows, 512)
    norm = 1.0 / math.sqrt(rows * n_chan_group)
    return pl.pallas_call(
        functools.partial(_time_dft_kernel, norm=norm),
        grid=(rows // tr, b),
        in_specs=[
            pl.BlockSpec((tr, rows), lambda i, bb: (i, 0)),
            pl.BlockSpec((tr, rows), lambda i, bb: (i, 0)),
            pl.BlockSpec((1, rows, 2 * df), lambda i, bb: (bb, 0, 0)),
            pl.BlockSpec((df, df), lambda i, bb: (0, 0)),
        ],
        out_specs=pl.BlockSpec((1, tr, df), lambda i, bb: (bb, i, 0)),
        out_shape=jax.ShapeDtypeStruct((b, rows, df), F32),
        compiler_params=_cparams(("arbitrary", "arbitrary")),
        name=f"time_dft_{rows}",
    )(ct, st, ab, w)


def _fourier(z, chan_cs, tables_ctx, tables_lat, w, *, n_ctx, col):
    b, t, _ = z.shape
    df = w.shape[0]
    tm = TOKEN_TILE
    ab = pl.pallas_call(
        _chan_dft_kernel,
        grid=(b, t // tm),
        in_specs=[
            pl.BlockSpec((1, tm, df), lambda bb, i: (bb, i, col // df)),
            pl.BlockSpec((df, 2 * df), lambda bb, i: (0, 0)),
        ],
        out_specs=pl.BlockSpec((1, tm, 2 * df), lambda bb, i: (bb, i, 0)),
        out_shape=jax.ShapeDtypeStruct((b, t, 2 * df), BF16),
        compiler_params=_cparams(("arbitrary", "arbitrary")),
        name="chan_dft",
    )(z, chan_cs)
    four_c = _fourier_segment(ab, tables_ctx, w, rows=n_ctx, n_chan_group=HEAD_DIM)
    four_l = _fourier_segment(ab[:, n_ctx:], tables_lat, w, rows=t - n_ctx, n_chan_group=HEAD_DIM)
    return jnp.concatenate([four_c, four_l], 1)


def _outproj_kernel(x_ref, a_ref, r_ref, p_ref, f_ref, m_ref, w_ref, g_ref, b_ref, o_ref, *, alpha):
    x = x_ref[0]
    gate = m_ref[0, 5:6, :]
    cat = jnp.concatenate([a_ref[0], r_ref[0], p_ref[0], f_ref[0]], -1).astype(BF16)
    y = _dot(cat, w_ref[...])
    o_ref[0] = _layer_norm(alpha * x + gate * y, g_ref[...], b_ref[...])


def _outproj(xs, att, rw, pool, four, mod_l, w_out, ln_g, ln_b, *, alpha, n_batch):
    b, t, d = xs.shape
    tm = TOKEN_TILE
    dm = att.shape[-1]
    branch = pl.BlockSpec((1, tm, dm), lambda bb, i: (bb, i, 0))
    return pl.pallas_call(
        functools.partial(_outproj_kernel, alpha=alpha),
        grid=(b, t // tm),
        in_specs=[
            pl.BlockSpec((1, tm, d), lambda bb, i: (bb, i, 0)), branch, branch, branch, branch,
            pl.BlockSpec((1, 9, d), _mod_row_map(n_batch)),
            pl.BlockSpec((4 * dm, d), lambda bb, i: (0, 0)),
            pl.BlockSpec((1, d), lambda bb, i: (0, 0)),
            pl.BlockSpec((1, d), lambda bb, i: (0, 0)),
        ],
        out_specs=pl.BlockSpec((1, tm, d), lambda bb, i: (bb, i, 0)),
        out_shape=jax.ShapeDtypeStruct((b, t, d), F32),
        compiler_params=_cparams(("arbitrary", "arbitrary")),
        name="outproj",
    )(xs, att, rw, pool, four, mod_l, w_out, ln_g.reshape(1, d), ln_b.reshape(1, d))


def _rope_tables(n_ctx, n_lat, n_heads):
    pos = jnp.arange(n_lat)
    rows = (pos // GRID_W).astype(F32)
    cols = (pos % GRID_W).astype(F32)
    n_pair_axis = HEAD_DIM // 4
    inv = ROPE_THETA ** (-jnp.arange(n_pair_axis, dtype=F32) / n_pair_axis)
    ang = jnp.concatenate([rows[:, None] * inv, cols[:, None] * inv], -1)
    cos = jnp.repeat(jnp.cos(ang), 2, axis=-1)
    sin = jnp.stack([-jnp.sin(ang), jnp.sin(ang)], -1).reshape(n_lat, HEAD_DIM)
    cos = jnp.concatenate([jnp.ones((n_ctx, HEAD_DIM), F32), cos], 0)
    sin = jnp.concatenate([jnp.zeros((n_ctx, HEAD_DIM), F32), sin], 0)
    return jnp.tile(cos, (1, n_heads)), jnp.tile(sin, (1, n_heads))


def _block_diag(blocks):
    n, r, c = blocks.shape
    eye = jnp.eye(n, dtype=blocks.dtype)
    return (eye[:, None, :, None] * blocks[:, :, None, :]).reshape(n * r, n * c)


def kernel(x, c, ctx, c_ctx, w_mod, b_mod, ln_g, ln_b, w_ffn_in, w_ffn_out, w_in, q_norm_g, k_norm_g, rwkv_mu, decay_w0, decay_w1, decay_w2, icl_a0, icl_a1, icl_a2, gate_g1, gate_g2, k_k, k_a, r_k, gn_g, gn_b, pool_w, pool_scale, fourier_w, w_out):
    n_batch, n_lat, d = x.shape
    n_ctx = ctx.shape[1]
    depth = w_mod.shape[0]
    if n_ctx != TOKEN_TILE or n_lat % TOKEN_TILE or n_batch + 1 > 8:
        raise ValueError("unsupported shapes")
    alpha = (2 * depth) ** 0.25
    d_rwkv = k_k.shape[-1]
    d_pool = pool_scale.shape[-1]
    d_four = fourier_w.shape[-1]
    d_in = w_in.shape[-1]
    d_att = d_in - 4 * d_rwkv - d_pool - d_four
    n_q = (d_att - 2 * N_KV_HEADS * HEAD_DIM) // HEAD_DIM

    xs = jnp.concatenate([ctx, x], 1)
    cc = jnp.zeros((8, d), F32).at[:n_batch].set(c).at[n_batch].set(c_ctx)
    mod = _modulation(cc, w_mod, b_mod)

    cos_t, sin_t = _rope_tables(n_ctx, n_lat, n_q)
    qg = jnp.tile(q_norm_g, (1, n_q))
    kg = jnp.tile(k_norm_g, (1, n_q))
    cg, sg = _dft_tables(HEAD_DIM)
    n_fg = d_four // HEAD_DIM
    eye_g = jnp.eye(n_fg, dtype=F32)
    chan_cs = jnp.concatenate([jnp.kron(eye_g, cg), jnp.kron(eye_g, sg)], 1)
    ct_c, st_c = _dft_tables(n_ctx)
    ct_l, st_l = _dft_tables(n_lat)
    tables_ctx = (ct_c.astype(BF16), (-st_c).astype(BF16))
    tables_lat = (ct_l.astype(BF16), (-st_l).astype(BF16))
    w_in_r = jnp.concatenate([w_in[:, :, d_att:d_att + 4 * d_rwkv], w_in[:, :, :d_att],
                              w_in[:, :, d_att + 4 * d_rwkv:]], -1).astype(BF16)
    att_col = 4 * d_rwkv
    pool_col = att_col + d_att
    four_col = pool_col + d_pool
    wgu = w_ffn_in.astype(BF16)
    wdn = w_ffn_out.astype(BF16)
    w_out_b = w_out.astype(BF16)

    for l in range(depth):
        mod_l = mod[l].reshape(8, 9, d)
        xs = _ffn(xs, mod_l, wgu[l, 0], wdn[l, 0], ln_g[l, 0], ln_b[l, 0], sub=0, alpha=alpha, n_batch=n_batch)
        z = _inproj(xs, mod_l, w_in_r[l], n_batch=n_batch)
        att = _attention(z, cos_t, sin_t, qg[l:l + 1], kg[l:l + 1], n_ctx=n_ctx, att_col=att_col)
        p = dict(rwkv_mu=rwkv_mu[l], decay_w0=decay_w0[l], decay_w1=decay_w1[l], decay_w2=decay_w2[l],
                 icl_a0=icl_a0[l], icl_a1=icl_a1[l], icl_a2=icl_a2[l], gate_g1=gate_g1[l], gate_g2=gate_g2[l],
                 k_k=k_k[l], k_a=k_a[l], r_k=r_k[l], gn_g=gn_g[l], gn_b=gn_b[l])
        rw = _rwkv(z, p, n_ctx=n_ctx)
        pool = _pool(z, _block_diag(pool_w[l]), pool_scale[l], n_ctx=n_ctx, col=pool_col)
        four = _fourier(z, chan_cs, tables_ctx, tables_lat, fourier_w[l], n_ctx=n_ctx, col=four_col)
        xs = _outproj(xs, att, rw, pool, four, mod_l, w_out_b[l], ln_g[l, 1], ln_b[l, 1],
                      alpha=alpha, n_batch=n_batch)
        xs = _ffn(xs, mod_l, wgu[l, 1], wdn[l, 1], ln_g[l, 2], ln_b[l, 2], sub=2, alpha=alpha, n_batch=n_batch)
    return xs[:, n_ctx:]
```

```python
import functools
import math

import jax
import jax.numpy as jnp
from jax import lax
from jax.experimental import pallas as pl
from jax.experimental.pallas import tpu as pltpu

F32 = jnp.float32
BF16 = jnp.bfloat16

HEAD_DIM = 64
GRID_W = 64
POOL_WINDOWS = (2, 4, 8, 16)
ROPE_THETA = 10000.0
LN_EPS = 1e-5
QK_EPS = 1e-6
GN_EPS = 64e-5
N_KV_HEADS = 2

TOKEN_TILE = 256
CHUNK = 64
FFN_COL_BLOCK = 1408
ATTN_ROW_SPLIT = 4
INVERSE_HI_LEVELS = 3
HALO = 8
VMEM_LIMIT = 56 * 1024 * 1024


def _cparams(sem):
    return pltpu.CompilerParams(dimension_semantics=sem, vmem_limit_bytes=VMEM_LIMIT)


def _dot(a, b):
    return jnp.dot(a, b, preferred_element_type=F32)


def _dot_nt(a, b):
    return lax.dot_general(a, b, (((1,), (1,)), ((), ())), preferred_element_type=F32)


def _dot_tn(a, b):
    return lax.dot_general(a, b, (((0,), (0,)), ((), ())), preferred_element_type=F32)


def _hi_lo(x):
    hi = x.astype(BF16)
    lo = (x - hi.astype(F32)).astype(BF16)
    return hi, lo


def _mm1(a, b, dot=_dot):
    return dot(a.astype(BF16), b.astype(BF16))


def _mm3(a, b, dot=_dot):
    a1, a2 = _hi_lo(a)
    b1, b2 = _hi_lo(b)
    return dot(a1, b1) + (dot(a1, b2) + dot(a2, b1))


def _mm_sel(sel_bf16, x):
    x1, x2 = _hi_lo(x)
    return _dot(sel_bf16, x1) + _dot(sel_bf16, x2)


def _mm_sel_r(x, sel_bf16):
    x1, x2 = _hi_lo(x)
    return _dot(x1, sel_bf16) + _dot(x2, sel_bf16)


def _group_ones(n, group):
    r = lax.broadcasted_iota(jnp.int32, (n, n), 0) // group
    c = lax.broadcasted_iota(jnp.int32, (n, n), 1) // group
    return r == c


def _layer_norm(t, g, b):
    mu = jnp.mean(t, -1, keepdims=True)
    tc = t - mu
    var = jnp.mean(tc * tc, -1, keepdims=True)
    return tc * lax.rsqrt(var + LN_EPS) * g + b


def _mod_kernel(c_ref, w_ref, b_ref, o_ref):
    c = c_ref[...]
    s = c * jax.nn.sigmoid(c)
    o_ref[0] = _mm1(s, w_ref[0]) + b_ref[0]


def _modulation(cc, w_mod, b_mod):
    depth, d, n = w_mod.shape
    tn = n // 4
    return pl.pallas_call(
        _mod_kernel,
        grid=(depth, n // tn),
        in_specs=[
            pl.BlockSpec((8, d), lambda l, j: (0, 0)),
            pl.BlockSpec((1, d, tn), lambda l, j: (l, 0, j)),
            pl.BlockSpec((1, 1, tn), lambda l, j: (l, 0, j)),
        ],
        out_specs=pl.BlockSpec((1, 8, tn), lambda l, j: (l, 0, j)),
        out_shape=jax.ShapeDtypeStruct((depth, 8, n), F32),
        compiler_params=_cparams(("arbitrary", "arbitrary")),
        name="modulation",
    )(cc, w_mod, b_mod.reshape(depth, 1, n))


def _mod_row_map(n_batch):
    return lambda b, i: (jnp.where(i == 0, n_batch, b), 0, 0)


def _mod_rows(m_ref, sub):
    return tuple(m_ref[0, 3 * sub + j:3 * sub + j + 1, :] for j in range(3))


def _swiglu_postnorm(x, mods, wgu_ref, wdn_ref, g, b, alpha):
    shift, scale, gate = mods
    dff = wdn_ref.shape[0]
    h = (x * (1.0 + scale) + shift).astype(BF16)
    acc = jnp.zeros(x.shape, F32)
    for j in range(dff // FFN_COL_BLOCK):
        lo, hi = j * FFN_COL_BLOCK, (j + 1) * FFN_COL_BLOCK
        gg = _dot(h, wgu_ref[:, lo:hi])
        uu = _dot(h, wgu_ref[:, dff + lo:dff + hi])
        act = (gg * jax.nn.sigmoid(gg) * uu).astype(BF16)
        acc = acc + _dot(act, wdn_ref[lo:hi, :])
    return _layer_norm(alpha * x + (0.5 * gate) * acc, g, b)


def _ffn_in_kernel(x_ref, m_ref, wgu_ref, wdn_ref, g_ref, b_ref, win_ref, o_ref, z_ref, *, alpha):
    xo = _swiglu_postnorm(x_ref[0], _mod_rows(m_ref, 0), wgu_ref, wdn_ref, g_ref[...], b_ref[...], alpha)
    o_ref[0] = xo
    shift, scale, _ = _mod_rows(m_ref, 1)
    z_ref[0] = _dot((xo * (1.0 + scale) + shift).astype(BF16), win_ref[...])


def _resident(shape):
    return pl.BlockSpec(shape, lambda bb, i: (0,) * len(shape), pipeline_mode=pl.Buffered(1))


def _ffn_in(xs, mod_l, wgu, wdn, ln_g, ln_b, w_in, *, alpha, n_batch):
    b, t, d = xs.shape
    dff = wdn.shape[0]
    n = w_in.shape[1]
    tm = TOKEN_TILE
    if dff % FFN_COL_BLOCK:
        raise ValueError("FFN width must be a multiple of FFN_COL_BLOCK")
    return pl.pallas_call(
        functools.partial(_ffn_in_kernel, alpha=alpha),
        grid=(b, t // tm),
        in_specs=[
            pl.BlockSpec((1, tm, d), lambda bb, i: (bb, i, 0)),
            pl.BlockSpec((1, 9, d), _mod_row_map(n_batch)),
            _resident((d, 2 * dff)), _resident((dff, d)), _resident((1, d)), _resident((1, d)),
            _resident((d, n)),
        ],
        out_specs=[pl.BlockSpec((1, tm, d), lambda bb, i: (bb, i, 0)),
                   pl.BlockSpec((1, tm, n), lambda bb, i: (bb, i, 0))],
        out_shape=[jax.ShapeDtypeStruct((b, t, d), F32), jax.ShapeDtypeStruct((b, t, n), F32)],
        compiler_params=_cparams(("arbitrary", "arbitrary")),
        name="ffn_in",
    )(xs, mod_l, wgu, wdn, ln_g.reshape(1, d), ln_b.reshape(1, d), w_in)


def _pair_swap(x):
    n = x.shape[-1]
    lane = lax.broadcasted_iota(jnp.int32, x.shape, 1)
    nxt = pltpu.roll(x, n - 1, 1)
    prv = pltpu.roll(x, 1, 1)
    return jnp.where(lane % 2 == 0, nxt, prv)


def _qkv_kernel(z_ref, cos_ref, sin_ref, qg_ref, kg_ref, q_ref, kt_ref, v_ref, *, n_q, n_kv):
    z = z_ref[0]
    dq, dk = n_q * HEAD_DIM, n_kv * HEAD_DIM
    zq, zk, zv = z[:, :dq], z[:, dq:dq + dk], z[:, dq + dk:dq + 2 * dk]
    cos, sin = cos_ref[...], sin_ref[...]
    xq = zq * qg_ref[...]
    yq = xq * cos + _pair_swap(xq) * sin
    xk = zk * kg_ref[...]
    yk = xk * cos[:, :dk] + _pair_swap(xk) * sin[:, :dk]
    q_scale = HEAD_DIM ** -0.5 * math.log2(math.e)
    for h in range(n_q):
        sl = slice(h * HEAD_DIM, (h + 1) * HEAD_DIM)
        ms = jnp.mean(zq[:, sl] * zq[:, sl], -1, keepdims=True)
        q_ref[0, h] = (yq[:, sl] * (lax.rsqrt(ms + QK_EPS) * q_scale)).astype(BF16)
    lane = lax.broadcasted_iota(jnp.int32, (1, dk), 1)
    inv = jnp.zeros(zk.shape, F32)
    for h in range(n_kv):
        sl = slice(h * HEAD_DIM, (h + 1) * HEAD_DIM)
        in_head = (lane // HEAD_DIM) == h
        ms = jnp.mean(zk[:, sl] * zk[:, sl], -1, keepdims=True)
        inv = jnp.where(in_head, lax.rsqrt(ms + QK_EPS), inv)
        v_ref[0, h] = jnp.where(in_head, zv, 1.0).astype(BF16)
    kt_ref[0] = (yk * inv).T.astype(BF16)


def _attn_kernel(q_ref, kt_ref, v_ref, o_ref, *, n_ctx, group):
    hkv = pl.program_id(1)
    i = pl.program_id(2)
    tq = q_ref.shape[2]
    q = q_ref[0].reshape(group * tq, HEAD_DIM)

    def run(kt, vv):
        parts = []
        for c in range(ATTN_ROW_SPLIT):
            rows = slice(c * (group * tq // ATTN_ROW_SPLIT), (c + 1) * (group * tq // ATTN_ROW_SPLIT))
            s = _dot(q[rows], kt)
            m = jnp.max(s, -1, keepdims=True)
            parts.append(_dot(jnp.exp2(s - m).astype(BF16), vv))
        o = jnp.concatenate(parts, 0)
        r = o / pltpu.roll(o, HEAD_DIM, 1)
        rs = pltpu.roll(r, HEAD_DIM, 1)
        lane = lax.broadcasted_iota(jnp.int32, (1, 2 * HEAD_DIM), 1)
        first = hkv % 2 == 0
        low = jnp.where(first, r[:tq], rs[:tq])
        high = jnp.where(first, rs[tq:], r[tq:])
        o_ref[0] = jnp.where(lane < HEAD_DIM, low, high)

    @pl.when(i == 0)
    def _():
        run(kt_ref[0, :, :n_ctx], v_ref[0, 0, :n_ctx])

    @pl.when(i > 0)
    def _():
        run(kt_ref[0], v_ref[0, 0])


def _attention(z, cos_t, sin_t, qg, kg, *, n_ctx, att_col):
    b, t, _ = z.shape
    tm = TOKEN_TILE
    n_q = qg.shape[1] // HEAD_DIM
    n_kv = N_KV_HEADS
    group = n_q // n_kv
    if group != 2 or n_kv != 2:
        raise ValueError("attention kernel is written for 2 kv heads x 2 query heads")
    wq = (n_q + 2 * n_kv) * HEAD_DIM
    q, kt, v = pl.pallas_call(
        functools.partial(_qkv_kernel, n_q=n_q, n_kv=n_kv),
        grid=(b, t // tm),
        in_specs=[
            pl.BlockSpec((1, tm, wq), lambda bb, i: (bb, i, att_col // wq)),
            pl.BlockSpec((tm, n_q * HEAD_DIM), lambda bb, i: (i, 0)),
            pl.BlockSpec((tm, n_q * HEAD_DIM), lambda bb, i: (i, 0)),
            pl.BlockSpec((1, n_q * HEAD_DIM), lambda bb, i: (0, 0)),
            pl.BlockSpec((1, n_kv * HEAD_DIM), lambda bb, i: (0, 0)),
        ],
        out_specs=[
            pl.BlockSpec((1, n_q, tm, HEAD_DIM), lambda bb, i: (bb, 0, i, 0)),
            pl.BlockSpec((1, n_kv * HEAD_DIM, tm), lambda bb, i: (bb, 0, i)),
            pl.BlockSpec((1, n_kv, tm, n_kv * HEAD_DIM), lambda bb, i: (bb, 0, i, 0)),
        ],
        out_shape=[
            jax.ShapeDtypeStruct((b, n_q, t, HEAD_DIM), BF16),
            jax.ShapeDtypeStruct((b, n_kv * HEAD_DIM, t), BF16),
            jax.ShapeDtypeStruct((b, n_kv, t, n_kv * HEAD_DIM), BF16),
        ],
        compiler_params=_cparams(("arbitrary", "arbitrary")),
        name="qkv_prep",
    )(z, cos_t, sin_t, qg, kg[:, :n_kv * HEAD_DIM])
    return pl.pallas_call(
        functools.partial(_attn_kernel, n_ctx=n_ctx, group=group),
        grid=(b, n_kv, t // tm),
        in_specs=[
            pl.BlockSpec((1, group, tm, HEAD_DIM), lambda bb, h, i: (bb, h, i, 0)),
            pl.BlockSpec((1, HEAD_DIM, t), lambda bb, h, i: (bb, h, 0)),
            pl.BlockSpec((1, 1, t, n_kv * HEAD_DIM), lambda bb, h, i: (bb, h, 0, 0)),
        ],
        out_specs=pl.BlockSpec((1, tm, group * HEAD_DIM), lambda bb, h, i: (bb, i, h)),
        out_shape=jax.ShapeDtypeStruct((b, t, n_q * HEAD_DIM), F32),
        compiler_params=_cparams(("arbitrary", "arbitrary", "arbitrary")),
        name="attention",
    )(q, kt, v)


def _halo_specs(width, col_block, rows_per_tile, n_tiles):
    r = rows_per_tile // HALO

    def prev_map(bb, i, *_):
        return (bb, jnp.maximum(i * r - 1, 0), col_block)

    def next_map(bb, i, *_):
        return (bb, jnp.minimum((i + 1) * r, n_tiles * r - 1), col_block)

    return (pl.BlockSpec((1, HALO, width), prev_map), pl.BlockSpec((1, HALO, width), next_map))


def _segment_flags(i, n_tiles):
    has_prev = i >= 2
    has_next = jnp.logical_and(i >= 1, i < n_tiles - 1)
    return has_prev, has_next


def _tri_inverse(lmat):
    n = lmat.shape[0]
    eye = (lax.broadcasted_iota(jnp.int32, (n, n), 0) == lax.broadcasted_iota(jnp.int32, (n, n), 1))
    acc = jnp.where(eye, 1.0, 0.0) + lmat
    power = lmat
    for level in range(int(math.log2(CHUNK)) - 1):
        mm = _mm3 if level < INVERSE_HI_LEVELS else _mm1
        power = mm(power, power)
        acc = acc + mm(power, acc)
    return acc


def _rwkv_chunk_kernel(z_ref, zp_ref, zn_ref, mu_ref, kk_ref, ka_ref, rk_ref, w0_ref, w1_ref, w2_ref,
                       a0_ref, a1_ref, a2_ref, g1_ref, g2_ref,
                       gm_ref, hm_ref, ry_ref, y0_ref, gate_ref, bonus_ref, *, dr):
    i = pl.program_id(1)
    d = pl.program_id(2)
    n_tiles = pl.num_programs(1)
    tm = z_ref.shape[1]
    z = z_ref[0]
    has_prev, has_next = _segment_flags(i, n_tiles)
    prow = jnp.where(has_prev, zp_ref[0, HALO - 1:HALO, :], 0.0)
    nrow = jnp.where(has_next, zn_ref[0, 0:1, :], 0.0)
    row = lax.broadcasted_iota(jnp.int32, (tm, 1), 0)
    zprev = jnp.where(row == 0, prow, pltpu.roll(z, 1, 0))
    znext = jnp.where(row == tm - 1, nrow, pltpu.roll(z, tm - 1, 0))
    dz = 0.5 * (zprev + znext) - z
    mu = mu_ref[...]
    r = z[:, 0:dr] + dz[:, 0:dr] * mu[0:1]
    k = z[:, dr:2 * dr] + dz[:, dr:2 * dr] * mu[1:2]
    v = z[:, 2 * dr:3 * dr] + dz[:, 2 * dr:3 * dr] * mu[2:3]
    zu, du = z[:, 3 * dr:4 * dr], dz[:, 3 * dr:4 * dr]
    xw = zu + du * mu[3:4]
    xa = zu + du * mu[4:5]
    xg = zu + du * mu[5:6]

    head_ones = jnp.where(_group_ones(dr, HEAD_DIM), 1.0, 0.0).astype(BF16)
    gate_ref[0] = _mm1(jax.nn.sigmoid(_mm1(xg, g1_ref[...])), g2_ref[...])
    kk = k * kk_ref[...]
    kk = kk / jnp.maximum(jnp.sqrt(_mm_sel_r(kk * kk, head_ones)), 1e-12)

    w_raw = w0_ref[pl.ds(d, 1), :] + _mm1(jnp.tanh(_mm1(xw, w1_ref[d])), w2_ref[d])
    logw = (-math.exp(-0.5)) * jax.nn.sigmoid(w_raw)
    aicl = jax.nn.sigmoid(a0_ref[pl.ds(d, 1), :] + _mm1(_mm1(xa, a1_ref[d]), a2_ref[d]))
    kd = k * (1.0 + (aicl - 1.0) * ka_ref[...])
    bvec = kk * aicl
    bonus_ref[0, 0] = _mm_sel_r(r * kd * rk_ref[...], head_ones) * v

    ri = lax.broadcasted_iota(jnp.int32, (tm, tm), 0)
    ci = lax.broadcasted_iota(jnp.int32, (tm, tm), 1)
    same_chunk = (ri // CHUNK) == (ci // CHUNK)
    before = (ci - ri) * (1 - 2 * d) < 0
    strict = jnp.logical_and(same_chunk, before)
    incl = jnp.logical_and(same_chunk, jnp.logical_or(before, ci == ri))
    cw = _mm_sel(jnp.where(incl, 1.0, 0.0).astype(BF16), logw)
    tot = _mm_sel(jnp.where(same_chunk, 1.0, 0.0).astype(BF16), logw)
    e_in = jnp.exp(cw)
    e_neg = jnp.exp(-cw)
    e_rem = jnp.exp(tot - cw)
    a_t = -kk * jnp.exp(cw - logw)
    r_t = r * e_in
    b_h = bvec * e_neg
    k_h = kd * e_neg
    b_b = bvec * e_rem
    k_b = kd * e_rem
    rhs_nt = jnp.concatenate([b_h, k_h], 0).astype(BF16)
    a_tb, r_tb, v_b = a_t.astype(BF16), r_t.astype(BF16), v.astype(BF16)
    lane = lax.broadcasted_iota(jnp.int32, (1, dr), 1)
    zero_b = jnp.zeros((), BF16)

    pq_nat = jnp.zeros((tm, 2 * dr), F32)
    ry = r_t
    y0 = jnp.zeros((tm, dr), F32)
    for h in range(dr // HEAD_DIM):
        mh = (lane // HEAD_DIM) == h
        a_m = jnp.where(mh, a_tb, zero_b)
        v_m = jnp.where(mh, v_b, zero_b)
        lhs = jnp.concatenate([a_m, jnp.where(mh, r_tb, zero_b)], 0)
        prod = _dot_nt(lhs, rhs_nt)
        lab = jnp.where(strict, prod[:tm, :tm], 0.0)
        lak = jnp.where(strict, prod[:tm, tm:], 0.0).astype(BF16)
        mr = jnp.where(jnp.concatenate([incl, incl], 1), prod[tm:, :], 0.0).astype(BF16)
        tinv = _tri_inverse(lab).astype(BF16)
        pq = _dot(tinv, jnp.concatenate([a_m, _dot(lak, v_m).astype(BF16)], 1))
        pq_nat = pq_nat + pq
        upd = _dot(mr[:, :tm], pq.astype(BF16))
        ry = ry + upd[:, :dr]
        y0 = y0 + (upd[:, dr:] + _dot(mr[:, tm:], v_m))
    ry_ref[0, 0] = ry
    y0_ref[0, 0] = y0

    p_b, q_b = pq_nat[:, :dr].astype(BF16), pq_nat[:, dr:].astype(BF16)
    bb_b, kb_b = b_b.astype(BF16), k_b.astype(BF16)
    same_head = _group_ones(dr, HEAD_DIM)
    eye = lax.broadcasted_iota(jnp.int32, (dr, dr), 0) == lax.broadcasted_iota(jnp.int32, (dr, dr), 1)
    for c in range(tm // CHUNK):
        sl = slice(c * CHUNK, (c + 1) * CHUNK)
        gmat = _dot_tn(bb_b[sl], p_b[sl])
        hmat = _dot_tn(jnp.concatenate([bb_b[sl], kb_b[sl]], 0), jnp.concatenate([q_b[sl], v_b[sl]], 0))
        decay = jnp.exp(tot[c * CHUNK:c * CHUNK + 1, :])
        gm_ref[0, 0, c] = jnp.where(same_head, gmat, 0.0) + jnp.where(eye, decay, 0.0)
        hm_ref[0, 0, c] = jnp.where(same_head, hmat, 0.0)


def _rwkv_state_kernel(gf_ref, hf_ref, gr_ref, hr_ref, sf_ref, sr_ref, st_ref):
    s = pl.program_id(0)

    @pl.when(s == 0)
    def _():
        st_ref[...] = jnp.zeros(st_ref.shape, F32)

    nb = gf_ref.shape[0]
    for b in range(nb):
        for d, (g_ref, h_ref, s_ref) in enumerate(((gf_ref, hf_ref, sf_ref), (gr_ref, hr_ref, sr_ref))):
            st = st_ref[b, d]
            s_ref[b, 0, 0] = st
            st_ref[b, d] = _mm3(g_ref[b, 0, 0], st) + h_ref[b, 0, 0]


def _rwkv_out_kernel(ry_ref, y0_ref, sf_ref, sr_ref, gate_ref, bonus_ref, gg_ref, gb_ref, o_ref):
    tm, dr = o_ref.shape[1], o_ref.shape[2]
    y = y0_ref[0, 0] + y0_ref[0, 1]
    parts = []
    for c in range(tm // CHUNK):
        sl = slice(c * CHUNK, (c + 1) * CHUNK)
        parts.append(_mm1(ry_ref[0, 0, sl, :], sf_ref[0, 0, c]) + _mm1(ry_ref[0, 1, sl, :], sr_ref[0, 0, c]))
    y = y + jnp.concatenate(parts, 0)
    head_ones = jnp.where(_group_ones(dr, HEAD_DIM), 1.0, 0.0).astype(BF16)
    mu = _mm_sel_r(y, head_ones) * (1.0 / HEAD_DIM)
    yc = y - mu
    var = _mm_sel_r(yc * yc, head_ones) * (1.0 / HEAD_DIM)
    yn = yc * lax.rsqrt(var + GN_EPS) * gg_ref[...] + gb_ref[...]
    o_ref[0] = (yn + (bonus_ref[0, 0] + bonus_ref[0, 1])) * gate_ref[0]


def _rwkv(z, p, *, n_ctx):
    b, t, _ = z.shape
    dr = p["k_k"].shape[-1]
    tm = TOKEN_TILE
    nt = t // tm
    cpt = tm // CHUNK
    nc = t // CHUNK
    nc_ctx = n_ctx // CHUNK
    wz = 4 * dr
    prev_spec, next_spec = _halo_specs(wz, 0, tm, nt)

    def full(shape):
        return pl.BlockSpec(shape, lambda bb, i, d: (0,) * len(shape))

    gm, hm, ry, y0, gate, bonus = pl.pallas_call(
        functools.partial(_rwkv_chunk_kernel, dr=dr),
        grid=(b, nt, 2),
        in_specs=[
            pl.BlockSpec((1, tm, wz), lambda bb, i, d: (bb, i, 0)), prev_spec, next_spec,
            full((6, dr)), full((1, dr)), full((1, dr)), full((1, dr)),
            full((2, dr)), full(p["decay_w1"].shape), full(p["decay_w2"].shape),
            full((2, dr)), full(p["icl_a1"].shape), full(p["icl_a2"].shape),
            full(p["gate_g1"].shape), full(p["gate_g2"].shape),
        ],
        out_specs=[
            pl.BlockSpec((1, 1, cpt, dr, dr), lambda bb, i, d: (bb, d, i, 0, 0)),
            pl.BlockSpec((1, 1, cpt, dr, dr), lambda bb, i, d: (bb, d, i, 0, 0)),
            pl.BlockSpec((1, 1, tm, dr), lambda bb, i, d: (bb, d, i, 0)),
            pl.BlockSpec((1, 1, tm, dr), lambda bb, i, d: (bb, d, i, 0)),
            pl.BlockSpec((1, tm, dr), lambda bb, i, d: (bb, i, 0)),
            pl.BlockSpec((1, 1, tm, dr), lambda bb, i, d: (bb, d, i, 0)),
        ],
        out_shape=[
            jax.ShapeDtypeStruct((b, 2, nc, dr, dr), F32),
            jax.ShapeDtypeStruct((b, 2, nc, dr, dr), F32),
            jax.ShapeDtypeStruct((b, 2, t, dr), F32),
            jax.ShapeDtypeStruct((b, 2, t, dr), F32),
            jax.ShapeDtypeStruct((b, t, dr), F32),
            jax.ShapeDtypeStruct((b, 2, t, dr), F32),
        ],
        compiler_params=_cparams(("arbitrary", "arbitrary", "arbitrary")),
        name="rwkv_chunks",
    )(z, z, z, p["rwkv_mu"], p["k_k"].reshape(1, dr), p["k_a"].reshape(1, dr), p["r_k"].reshape(1, dr),
      p["decay_w0"], p["decay_w1"], p["decay_w2"], p["icl_a0"], p["icl_a1"], p["icl_a2"],
      p["gate_g1"], p["gate_g2"])

    def fwd_map(s):
        return (0, 0, s, 0, 0)

    def rev_map(s):
        return (0, 1, jnp.where(s < nc_ctx, nc_ctx - 1 - s, nc + nc_ctx - 1 - s), 0, 0)

    blk = (b, 1, 1, dr, dr)
    s_fwd, s_rev = pl.pallas_call(
        _rwkv_state_kernel,
        grid=(nc,),
        in_specs=[pl.BlockSpec(blk, fwd_map), pl.BlockSpec(blk, fwd_map),
                  pl.BlockSpec(blk, rev_map), pl.BlockSpec(blk, rev_map)],
        out_specs=[pl.BlockSpec(blk, lambda s: (0, 0, s, 0, 0)),
                   pl.BlockSpec(blk, lambda s: (0, 0, rev_map(s)[2], 0, 0))],
        out_shape=[jax.ShapeDtypeStruct((b, 1, nc, dr, dr), F32),
                   jax.ShapeDtypeStruct((b, 1, nc, dr, dr), F32)],
        scratch_shapes=[pltpu.VMEM((b, 2, dr, dr), F32)],
        compiler_params=_cparams(("arbitrary",)),
        name="rwkv_state",
    )(gm, hm, gm, hm)

    return pl.pallas_call(
        _rwkv_out_kernel,
        grid=(b, nt),
        in_specs=[
            pl.BlockSpec((1, 2, tm, dr), lambda bb, i: (bb, 0, i, 0)),
            pl.BlockSpec((1, 2, tm, dr), lambda bb, i: (bb, 0, i, 0)),
            pl.BlockSpec((1, 1, cpt, dr, dr), lambda bb, i: (bb, 0, i, 0, 0)),
            pl.BlockSpec((1, 1, cpt, dr, dr), lambda bb, i: (bb, 0, i, 0, 0)),
            pl.BlockSpec((1, tm, dr), lambda bb, i: (bb, i, 0)),
            pl.BlockSpec((1, 2, tm, dr), lambda bb, i: (bb, 0, i, 0)),
            pl.BlockSpec((1, dr), lambda bb, i: (0, 0)),
            pl.BlockSpec((1, dr), lambda bb, i: (0, 0)),
        ],
        out_specs=pl.BlockSpec((1, tm, dr), lambda bb, i: (bb, i, 0)),
        out_shape=jax.ShapeDtypeStruct((b, t, dr), F32),
        compiler_params=_cparams(("arbitrary", "arbitrary")),
        name="rwkv_out",
    )(ry, y0, s_fwd, s_rev, gate, bonus, p["gn_g"].reshape(1, dr), p["gn_b"].reshape(1, dr))


def _pool_kernel(z_ref, zp_ref, zn_ref, w_ref, sc_ref, o_ref, *, n_ctx):
    i = pl.program_id(1)
    n_tiles = pl.num_programs(1)
    tm, dp = z_ref.shape[1], z_ref.shape[2]
    has_prev, has_next = _segment_flags(i, n_tiles)
    x = z_ref[0]
    xe = jnp.concatenate([jnp.where(has_prev, zp_ref[0], 0.0), x, jnp.where(has_next, zn_ref[0], 0.0)], 0)
    ne = tm + 2 * HALO

    def back(a, s):
        return pltpu.roll(a, s, 0)

    def ahead(a, s):
        return pltpu.roll(a, ne - s, 0)

    d2 = xe + back(xe, 1)
    d4 = d2 + back(d2, 2)
    d8 = d4 + back(d4, 4)
    e2 = xe + ahead(xe, 1)
    e4 = e2 + ahead(e2, 2)
    f1 = ahead(xe, 1)
    f3 = f1 + ahead(e2, 2)
    f7 = f3 + ahead(e4, 4)
    sums = (d2, d2 + back(xe, 2) + f1, d4 + back(xe, 4) + f3, d8 + back(xe, 8) + f7)

    tglob = i * tm + lax.broadcasted_iota(jnp.int32, (tm, 1), 0)
    n_lat = n_tiles * tm - n_ctx
    pos = jnp.where(i == 0, tglob, tglob - n_ctx)
    seg = jnp.where(i == 0, n_ctx, n_lat)
    lane = lax.broadcasted_iota(jnp.int32, (1, dp), 1)
    gw = dp // len(POOL_WINDOWS)
    pooled = jnp.zeros((tm, dp), F32)
    for g, win in enumerate(POOL_WINDOWS):
        nb, nf = win // 2, win - win // 2 - 1
        cnt = (jnp.minimum(pos, nb) + 1 + jnp.minimum(seg - 1 - pos, nf)).astype(F32)
        mean = sums[g][HALO:HALO + tm] / cnt
        pooled = jnp.where((lane // gw) == g, mean - x, pooled)
    o_ref[0] = _mm1(pooled, w_ref[...]) * sc_ref[...]


def _pool(z, w_bd, scale, *, n_ctx, col):
    b, t, _ = z.shape
    dp = w_bd.shape[0]
    tm = TOKEN_TILE
    prev_spec, next_spec = _halo_specs(dp, col // dp, tm, t // tm)
    return pl.pallas_call(
        functools.partial(_pool_kernel, n_ctx=n_ctx),
        grid=(b, t // tm),
        in_specs=[
            pl.BlockSpec((1, tm, dp), lambda bb, i: (bb, i, col // dp)), prev_spec, next_spec,
            pl.BlockSpec((dp, dp), lambda bb, i: (0, 0)),
            pl.BlockSpec((1, dp), lambda bb, i: (0, 0)),
        ],
        out_specs=pl.BlockSpec((1, tm, dp), lambda bb, i: (bb, i, 0)),
        out_shape=jax.ShapeDtypeStruct((b, t, dp), F32),
        compiler_params=_cparams(("arbitrary", "arbitrary")),
        name="pool",
    )(z, z, z, w_bd, scale.reshape(1, dp))


def _chan_dft_kernel(z_ref, cs_ref, o_ref):
    o_ref[0] = _mm3(z_ref[0], cs_ref[...]).astype(BF16)


def _time_dft_kernel(ct_ref, st_ref, ab_ref, w_ref, o_ref, *, norm):
    df = w_ref.shape[0]
    ab = ab_ref[0]
    f = (_dot(ct_ref[...], ab[:, :df]) + _dot(st_ref[...], ab[:, df:])) * norm
    o_ref[0] = _mm1(f, w_ref[...])


def _dft_tables(n):
    n2 = 1 << (int(math.log2(n)) // 2)
    n1 = n // n2
    if n1 * n2 != n:
        raise ValueError("DFT length must be a power of two")
    tp = jnp.arange(n, dtype=jnp.int32)[:, None]
    alpha = ((tp * jnp.arange(n1, dtype=jnp.int32)[None, :]) % n1).astype(F32) * (2.0 * math.pi / n1)
    beta = ((tp * jnp.arange(n2, dtype=jnp.int32)[None, :]) % n).astype(F32) * (2.0 * math.pi / n)
    ca, sa = jnp.cos(alpha)[:, :, None], jnp.sin(alpha)[:, :, None]
    cb, sb = jnp.cos(beta)[:, None, :], jnp.sin(beta)[:, None, :]
    return (ca * cb - sa * sb).reshape(n, n), (sa * cb + ca * sb).reshape(n, n)


def _fourier_segment(ab, tables, w, *, rows, n_chan_group):
    b = ab.shape[0]
    df = w.shape[0]
    ct, st = tables
    tr = min(rows, 512)
    norm = 1.0 / math.sqrt(rows * n_chan_group)
    return pl.pallas_call(
        functools.partial(_time_dft_kernel, norm=norm),
        grid=(rows // tr, b),
        in_specs=[
            pl.BlockSpec((tr, rows), lambda i, bb: (i, 0)),
            pl.BlockSpec((tr, rows), lambda i, bb: (i, 0)),
            pl.BlockSpec((1, rows, 2 * df), lambda i, bb: (bb, 0, 0)),
            pl.BlockSpec((df, df), lambda i, bb: (0, 0)),
        ],
        out_specs=pl.BlockSpec((1, tr, df), lambda i, bb: (bb, i, 0)),
        out_shape=jax.ShapeDtypeStruct((b, rows, df), F32),
        compiler_params=_cparams(("arbitrary", "arbitrary")),
        name=f"time_dft_{rows}",
    )(ct, st, ab, w)


def _fourier(z, chan_cs, tables_ctx, tables_lat, w, *, n_ctx, col):
    b, t, _ = z.shape
    df = w.shape[0]
    tm = TOKEN_TILE
    ab = pl.pallas_call(
        _chan_dft_kernel,
        grid=(b, t // tm),
        in_specs=[
            pl.BlockSpec((1, tm, df), lambda bb, i: (bb, i, col // df)),
            pl.BlockSpec((df, 2 * df), lambda bb, i: (0, 0)),
        ],
        out_specs=pl.BlockSpec((1, tm, 2 * df), lambda bb, i: (bb, i, 0)),
        out_shape=jax.ShapeDtypeStruct((b, t, 2 * df), BF16),
        compiler_params=_cparams(("arbitrary", "arbitrary")),
        name="chan_dft",
    )(z, chan_cs)
    four_c = _fourier_segment(ab, tables_ctx, w, rows=n_ctx, n_chan_group=HEAD_DIM)
    four_l = _fourier_segment(ab[:, n_ctx:], tables_lat, w, rows=t - n_ctx, n_chan_group=HEAD_DIM)
    return jnp.concatenate([four_c, four_l], 1)


def _out_ffn_kernel(x_ref, a_ref, r_ref, p_ref, f_ref, m_ref, wout_ref, g1_ref, b1_ref, wgu_ref, wdn_ref,
                    g2_ref, b2_ref, o_ref, *, alpha):
    x = x_ref[0]
    gate = _mod_rows(m_ref, 1)[2]
    cat = jnp.concatenate([a_ref[0], r_ref[0], p_ref[0], f_ref[0]], -1).astype(BF16)
    x1 = _layer_norm(alpha * x + gate * _dot(cat, wout_ref[...]), g1_ref[...], b1_ref[...])
    o_ref[0] = _swiglu_postnorm(x1, _mod_rows(m_ref, 2), wgu_ref, wdn_ref, g2_ref[...], b2_ref[...], alpha)


def _out_ffn(xs, att, rw, pool, four, mod_l, w_out, ln_g1, ln_b1, wgu, wdn, ln_g2, ln_b2, *, alpha, n_batch,
             latent_only):
    b, t, d = xs.shape
    tm = TOKEN_TILE
    dm = att.shape[-1]
    dff = wdn.shape[0]
    skip = 1 if latent_only else 0
    rows = lambda bb, i: (bb, i + skip, 0)
    mod_map = (lambda bb, i: (bb, 0, 0)) if latent_only else _mod_row_map(n_batch)
    branch = pl.BlockSpec((1, tm, dm), rows)
    return pl.pallas_call(
        functools.partial(_out_ffn_kernel, alpha=alpha),
        grid=(b, t // tm - skip),
        in_specs=[
            pl.BlockSpec((1, tm, d), rows), branch, branch, branch, branch,
            pl.BlockSpec((1, 9, d), mod_map),
            _resident((4 * dm, d)), _resident((1, d)), _resident((1, d)),
            _resident((d, 2 * dff)), _resident((dff, d)), _resident((1, d)), _resident((1, d)),
        ],
        out_specs=pl.BlockSpec((1, tm, d), lambda bb, i: (bb, i, 0)),
        out_shape=jax.ShapeDtypeStruct((b, t - skip * tm, d), F32),
        compiler_params=_cparams(("arbitrary", "arbitrary")),
        name="out_ffn",
    )(xs, att, rw, pool, four, mod_l, w_out, ln_g1.reshape(1, d), ln_b1.reshape(1, d), wgu, wdn,
      ln_g2.reshape(1, d), ln_b2.reshape(1, d))


def _rope_tables(n_ctx, n_lat, n_heads):
    pos = jnp.arange(n_lat)
    rows = (pos // GRID_W).astype(F32)
    cols = (pos % GRID_W).astype(F32)
    n_pair_axis = HEAD_DIM // 4
    inv = ROPE_THETA ** (-jnp.arange(n_pair_axis, dtype=F32) / n_pair_axis)
    ang = jnp.concatenate([rows[:, None] * inv, cols[:, None] * inv], -1)
    cos = jnp.repeat(jnp.cos(ang), 2, axis=-1)
    sin = jnp.stack([-jnp.sin(ang), jnp.sin(ang)], -1).reshape(n_lat, HEAD_DIM)
    cos = jnp.concatenate([jnp.ones((n_ctx, HEAD_DIM), F32), cos], 0)
    sin = jnp.concatenate([jnp.zeros((n_ctx, HEAD_DIM), F32), sin], 0)
    return jnp.tile(cos, (1, n_heads)), jnp.tile(sin, (1, n_heads))


def _block_diag(blocks):
    n, r, c = blocks.shape
    eye = jnp.eye(n, dtype=blocks.dtype)
    return (eye[:, None, :, None] * blocks[:, :, None, :]).reshape(n * r, n * c)


def kernel(x, c, ctx, c_ctx, w_mod, b_mod, ln_g, ln_b, w_ffn_in, w_ffn_out, w_in, q_norm_g, k_norm_g, rwkv_mu, decay_w0, decay_w1, decay_w2, icl_a0, icl_a1, icl_a2, gate_g1, gate_g2, k_k, k_a, r_k, gn_g, gn_b, pool_w, pool_scale, fourier_w, w_out):
    n_batch, n_lat, d = x.shape
    n_ctx = ctx.shape[1]
    depth = w_mod.shape[0]
    if n_ctx != TOKEN_TILE or n_lat % TOKEN_TILE or n_batch + 1 > 8:
        raise ValueError("unsupported shapes")
    alpha = (2 * depth) ** 0.25
    d_rwkv = k_k.shape[-1]
    d_pool = pool_scale.shape[-1]
    d_four = fourier_w.shape[-1]
    d_in = w_in.shape[-1]
    d_att = d_in - 4 * d_rwkv - d_pool - d_four
    n_q = (d_att - 2 * N_KV_HEADS * HEAD_DIM) // HEAD_DIM

    xs = jnp.concatenate([ctx, x], 1)
    cc = jnp.zeros((8, d), F32).at[:n_batch].set(c).at[n_batch].set(c_ctx)
    mod = _modulation(cc, w_mod, b_mod)

    cos_t, sin_t = _rope_tables(n_ctx, n_lat, n_q)
    qg = jnp.tile(q_norm_g, (1, n_q))
    kg = jnp.tile(k_norm_g, (1, n_q))
    cg, sg = _dft_tables(HEAD_DIM)
    n_fg = d_four // HEAD_DIM
    eye_g = jnp.eye(n_fg, dtype=F32)
    chan_cs = jnp.concatenate([jnp.kron(eye_g, cg), jnp.kron(eye_g, sg)], 1)
    ct_c, st_c = _dft_tables(n_ctx)
    ct_l, st_l = _dft_tables(n_lat)
    tables_ctx = (ct_c.astype(BF16), (-st_c).astype(BF16))
    tables_lat = (ct_l.astype(BF16), (-st_l).astype(BF16))
    w_in_r = jnp.concatenate([w_in[:, :, d_att:d_att + 4 * d_rwkv], w_in[:, :, :d_att],
                              w_in[:, :, d_att + 4 * d_rwkv:]], -1).astype(BF16)
    att_col = 4 * d_rwkv
    pool_col = att_col + d_att
    four_col = pool_col + d_pool
    wgu = w_ffn_in.astype(BF16)
    wdn = w_ffn_out.astype(BF16)
    w_out_b = w_out.astype(BF16)

    for l in range(depth):
        last = l == depth - 1
        mod_l = mod[l].reshape(8, 9, d)
        xs, z = _ffn_in(xs, mod_l, wgu[l, 0], wdn[l, 0], ln_g[l, 0], ln_b[l, 0], w_in_r[l],
                        alpha=alpha, n_batch=n_batch)
        att = _attention(z, cos_t, sin_t, qg[l:l + 1], kg[l:l + 1], n_ctx=n_ctx, att_col=att_col)
        p = dict(rwkv_mu=rwkv_mu[l], decay_w0=decay_w0[l], decay_w1=decay_w1[l], decay_w2=decay_w2[l],
                 icl_a0=icl_a0[l], icl_a1=icl_a1[l], icl_a2=icl_a2[l], gate_g1=gate_g1[l], gate_g2=gate_g2[l],
                 k_k=k_k[l], k_a=k_a[l], r_k=r_k[l], gn_g=gn_g[l], gn_b=gn_b[l])
        rw = _rwkv(z, p, n_ctx=n_ctx)
        pool = _pool(z, _block_diag(pool_w[l]), pool_scale[l], n_ctx=n_ctx, col=pool_col)
        four = _fourier(z, chan_cs, tables_ctx, tables_lat, fourier_w[l], n_ctx=n_ctx, col=four_col)
        xs = _out_ffn(xs, att, rw, pool, four, mod_l, w_out_b[l], ln_g[l, 1], ln_b[l, 1], wgu[l, 1], wdn[l, 1],
                      ln_g[l, 2], ln_b[l, 2], alpha=alpha, n_batch=n_batch, latent_only=last)
    return xs
```

```python
import functools
import math

import jax
import jax.numpy as jnp
from jax import lax
from jax.experimental import pallas as pl
from jax.experimental.pallas import tpu as pltpu

F32 = jnp.float32
BF16 = jnp.bfloat16

HEAD_DIM = 64
GRID_W = 64
POOL_WINDOWS = (2, 4, 8, 16)
ROPE_THETA = 10000.0
LN_EPS = 1e-5
QK_EPS = 1e-6
GN_EPS = 64e-5
N_KV_HEADS = 2

TOKEN_TILE = 256
CHUNK = 64
FFN_COL_BLOCK = 1408
ATTN_ROW_SPLIT = 4
INVERSE_HI_LEVELS = 3
HALO = 8
VMEM_LIMIT = 56 * 1024 * 1024


def _cparams(sem, **kw):
    return pltpu.CompilerParams(dimension_semantics=sem, vmem_limit_bytes=VMEM_LIMIT, **kw)


def _dot(a, b):
    return jnp.dot(a, b, preferred_element_type=F32)


def _dot_nt(a, b):
    return lax.dot_general(a, b, (((1,), (1,)), ((), ())), preferred_element_type=F32)


def _dot_tn(a, b):
    return lax.dot_general(a, b, (((0,), (0,)), ((), ())), preferred_element_type=F32)


def _hi_lo(x):
    hi = x.astype(BF16)
    lo = (x - hi.astype(F32)).astype(BF16)
    return hi, lo


def _mm1(a, b, dot=_dot):
    return dot(a.astype(BF16), b.astype(BF16))


def _mm3(a, b):
    m = a.shape[0]
    a1, a2 = _hi_lo(a)
    b1, b2 = _hi_lo(b)
    top = _dot(jnp.concatenate([a1, a2], 0), b1)
    return top[:m] + (top[m:] + _dot(a1, b2))


def _mm_sel(sel_bf16, x):
    x1, x2 = _hi_lo(x)
    return _dot(sel_bf16, x1) + _dot(sel_bf16, x2)


def _mm_sel_r(x, sel_bf16):
    x1, x2 = _hi_lo(x)
    return _dot(x1, sel_bf16) + _dot(x2, sel_bf16)


def _group_ones(n, group):
    r = lax.broadcasted_iota(jnp.int32, (n, n), 0) // group
    c = lax.broadcasted_iota(jnp.int32, (n, n), 1) // group
    return r == c


def _layer_norm(t, g, b):
    mu = jnp.mean(t, -1, keepdims=True)
    tc = t - mu
    var = jnp.mean(tc * tc, -1, keepdims=True)
    return tc * lax.rsqrt(var + LN_EPS) * g + b


def _mod_kernel(c_ref, w_ref, b_ref, o_ref):
    c = c_ref[...]
    s = c * jax.nn.sigmoid(c)
    o_ref[0] = _mm1(s, w_ref[0]) + b_ref[0]


def _modulation(cc, w_mod, b_mod):
    depth, d, n = w_mod.shape
    tn = n // 4
    return pl.pallas_call(
        _mod_kernel,
        grid=(depth, n // tn),
        in_specs=[
            pl.BlockSpec((8, d), lambda l, j: (0, 0)),
            pl.BlockSpec((1, d, tn), lambda l, j: (l, 0, j)),
            pl.BlockSpec((1, 1, tn), lambda l, j: (l, 0, j)),
        ],
        out_specs=pl.BlockSpec((1, 8, tn), lambda l, j: (l, 0, j)),
        out_shape=jax.ShapeDtypeStruct((depth, 8, n), F32),
        compiler_params=_cparams(("arbitrary", "arbitrary")),
        name="modulation",
    )(cc, w_mod, b_mod.reshape(depth, 1, n))


def _mod_row_map(n_batch):
    return lambda b, i: (jnp.where(i == 0, n_batch, b), 0, 0)


def _mod_rows(m_ref, sub):
    return tuple(m_ref[0, 3 * sub + j:3 * sub + j + 1, :] for j in range(3))


def _swiglu_postnorm(x, mods, wgu_ref, wdn_ref, g, b, alpha):
    shift, scale, gate = mods
    dff = wdn_ref.shape[0]
    h = (x * (1.0 + scale) + shift).astype(BF16)
    acc = jnp.zeros(x.shape, F32)
    for j in range(dff // FFN_COL_BLOCK):
        lo, hi = j * FFN_COL_BLOCK, (j + 1) * FFN_COL_BLOCK
        gg = _dot(h, wgu_ref[:, lo:hi])
        uu = _dot(h, wgu_ref[:, dff + lo:dff + hi])
        act = (gg * jax.nn.sigmoid(gg) * uu).astype(BF16)
        acc = acc + _dot(act, wdn_ref[lo:hi, :])
    return _layer_norm(alpha * x + (0.5 * gate) * acc, g, b)


def _ffn_in_kernel(x_ref, m_ref, wgu_ref, wdn_ref, g_ref, b_ref, win_ref, o_ref, z_ref, *, alpha):
    xo = _swiglu_postnorm(x_ref[0], _mod_rows(m_ref, 0), wgu_ref, wdn_ref, g_ref[...], b_ref[...], alpha)
    o_ref[0] = xo
    shift, scale, _ = _mod_rows(m_ref, 1)
    z_ref[0] = _dot((xo * (1.0 + scale) + shift).astype(BF16), win_ref[...])


def _resident(shape):
    return pl.BlockSpec(shape, lambda bb, i: (0,) * len(shape), pipeline_mode=pl.Buffered(1))


def _ffn_in(xs, mod_l, wgu, wdn, ln_g, ln_b, w_in, *, alpha, n_batch):
    b, t, d = xs.shape
    dff = wdn.shape[0]
    n = w_in.shape[1]
    tm = TOKEN_TILE
    if dff % FFN_COL_BLOCK:
        raise ValueError("FFN width must be a multiple of FFN_COL_BLOCK")
    return pl.pallas_call(
        functools.partial(_ffn_in_kernel, alpha=alpha),
        grid=(b, t // tm),
        in_specs=[
            pl.BlockSpec((1, tm, d), lambda bb, i: (bb, i, 0)),
            pl.BlockSpec((1, 9, d), _mod_row_map(n_batch)),
            _resident((d, 2 * dff)), _resident((dff, d)), _resident((1, d)), _resident((1, d)),
            _resident((d, n)),
        ],
        out_specs=[pl.BlockSpec((1, tm, d), lambda bb, i: (bb, i, 0)),
                   pl.BlockSpec((1, tm, n), lambda bb, i: (bb, i, 0))],
        out_shape=[jax.ShapeDtypeStruct((b, t, d), F32), jax.ShapeDtypeStruct((b, t, n), F32)],
        compiler_params=_cparams(("arbitrary", "arbitrary")),
        name="ffn_in",
    )(xs, mod_l, wgu, wdn, ln_g.reshape(1, d), ln_b.reshape(1, d), w_in)


def _pair_swap(x):
    n = x.shape[-1]
    lane = lax.broadcasted_iota(jnp.int32, x.shape, 1)
    nxt = pltpu.roll(x, n - 1, 1)
    prv = pltpu.roll(x, 1, 1)
    return jnp.where(lane % 2 == 0, nxt, prv)


def _qkv_kernel(z_ref, cos_ref, sin_ref, qg_ref, kg_ref, q_ref, kt_ref, v_ref, *, n_q, n_kv):
    z = z_ref[0]
    dq, dk = n_q * HEAD_DIM, n_kv * HEAD_DIM
    zq, zk, zv = z[:, :dq], z[:, dq:dq + dk], z[:, dq + dk:dq + 2 * dk]
    cos, sin = cos_ref[...], sin_ref[...]
    xq = zq * qg_ref[...]
    yq = xq * cos + _pair_swap(xq) * sin
    xk = zk * kg_ref[...]
    yk = xk * cos[:, :dk] + _pair_swap(xk) * sin[:, :dk]
    q_scale = HEAD_DIM ** -0.5 * math.log2(math.e)
    for h in range(n_q):
        sl = slice(h * HEAD_DIM, (h + 1) * HEAD_DIM)
        ms = jnp.mean(zq[:, sl] * zq[:, sl], -1, keepdims=True)
        q_ref[0, h] = (yq[:, sl] * (lax.rsqrt(ms + QK_EPS) * q_scale)).astype(BF16)
    lane = lax.broadcasted_iota(jnp.int32, (1, dk), 1)
    inv = jnp.zeros(zk.shape, F32)
    for h in range(n_kv):
        sl = slice(h * HEAD_DIM, (h + 1) * HEAD_DIM)
        in_head = (lane // HEAD_DIM) == h
        ms = jnp.mean(zk[:, sl] * zk[:, sl], -1, keepdims=True)
        inv = jnp.where(in_head, lax.rsqrt(ms + QK_EPS), inv)
        v_ref[0, h] = jnp.where(in_head, zv, 1.0).astype(BF16)
    kt_ref[0] = (yk * inv).T.astype(BF16)


def _attn_kernel(q_ref, kt_ref, v_ref, o_ref, *, n_ctx, group):
    hkv = pl.program_id(1)
    i = pl.program_id(2)
    tq = q_ref.shape[2]
    q = q_ref[0].reshape(group * tq, HEAD_DIM)

    def run(kt, vv):
        parts = []
        for c in range(ATTN_ROW_SPLIT):
            rows = slice(c * (group * tq // ATTN_ROW_SPLIT), (c + 1) * (group * tq // ATTN_ROW_SPLIT))
            s = _dot(q[rows], kt)
            m = jnp.max(s, -1, keepdims=True)
            parts.append(_dot(jnp.exp2(s - m).astype(BF16), vv))
        o = jnp.concatenate(parts, 0)
        r = o / pltpu.roll(o, HEAD_DIM, 1)
        rs = pltpu.roll(r, HEAD_DIM, 1)
        lane = lax.broadcasted_iota(jnp.int32, (1, 2 * HEAD_DIM), 1)
        first = hkv % 2 == 0
        low = jnp.where(first, r[:tq], rs[:tq])
        high = jnp.where(first, rs[tq:], r[tq:])
        o_ref[0] = jnp.where(lane < HEAD_DIM, low, high)

    @pl.when(i == 0)
    def _():
        run(kt_ref[0, :, :n_ctx], v_ref[0, 0, :n_ctx])

    @pl.when(i > 0)
    def _():
        run(kt_ref[0], v_ref[0, 0])


def _attention(z, cos_t, sin_t, qg, kg, *, n_ctx, att_col):
    b, t, _ = z.shape
    tm = TOKEN_TILE
    n_q = qg.shape[1] // HEAD_DIM
    n_kv = N_KV_HEADS
    group = n_q // n_kv
    if group != 2 or n_kv != 2:
        raise ValueError("attention kernel is written for 2 kv heads x 2 query heads")
    wq = (n_q + 2 * n_kv) * HEAD_DIM
    q, kt, v = pl.pallas_call(
        functools.partial(_qkv_kernel, n_q=n_q, n_kv=n_kv),
        grid=(b, t // tm),
        in_specs=[
            pl.BlockSpec((1, tm, wq), lambda bb, i: (bb, i, att_col // wq)),
            pl.BlockSpec((tm, n_q * HEAD_DIM), lambda bb, i: (i, 0)),
            pl.BlockSpec((tm, n_q * HEAD_DIM), lambda bb, i: (i, 0)),
            pl.BlockSpec((1, n_q * HEAD_DIM), lambda bb, i: (0, 0)),
            pl.BlockSpec((1, n_kv * HEAD_DIM), lambda bb, i: (0, 0)),
        ],
        out_specs=[
            pl.BlockSpec((1, n_q, tm, HEAD_DIM), lambda bb, i: (bb, 0, i, 0)),
            pl.BlockSpec((1, n_kv * HEAD_DIM, tm), lambda bb, i: (bb, 0, i)),
            pl.BlockSpec((1, n_kv, tm, n_kv * HEAD_DIM), lambda bb, i: (bb, 0, i, 0)),
        ],
        out_shape=[
            jax.ShapeDtypeStruct((b, n_q, t, HEAD_DIM), BF16),
            jax.ShapeDtypeStruct((b, n_kv * HEAD_DIM, t), BF16),
            jax.ShapeDtypeStruct((b, n_kv, t, n_kv * HEAD_DIM), BF16),
        ],
        compiler_params=_cparams(("arbitrary", "arbitrary")),
        name="qkv_prep",
    )(z, cos_t, sin_t, qg, kg[:, :n_kv * HEAD_DIM])
    return pl.pallas_call(
        functools.partial(_attn_kernel, n_ctx=n_ctx, group=group),
        grid=(b, n_kv, t // tm),
        in_specs=[
            pl.BlockSpec((1, group, tm, HEAD_DIM), lambda bb, h, i: (bb, h, i, 0)),
            pl.BlockSpec((1, HEAD_DIM, t), lambda bb, h, i: (bb, h, 0)),
            pl.BlockSpec((1, 1, t, n_kv * HEAD_DIM), lambda bb, h, i: (bb, h, 0, 0)),
        ],
        out_specs=pl.BlockSpec((1, tm, group * HEAD_DIM), lambda bb, h, i: (bb, i, h)),
        out_shape=jax.ShapeDtypeStruct((b, t, n_q * HEAD_DIM), F32),
        compiler_params=_cparams(("arbitrary", "arbitrary", "arbitrary")),
        name="attention",
    )(q, kt, v)


def _halo_specs(width, col_block, rows_per_tile, n_tiles, tile_of=lambda i: i):
    r = rows_per_tile // HALO

    def prev_map(bb, i, *_):
        return (bb, jnp.maximum(tile_of(i) * r - 1, 0), col_block)

    def next_map(bb, i, *_):
        return (bb, jnp.minimum((tile_of(i) + 1) * r, n_tiles * r - 1), col_block)

    return (pl.BlockSpec((1, HALO, width), prev_map), pl.BlockSpec((1, HALO, width), next_map))


def _segment_flags(i, n_tiles):
    has_prev = i >= 2
    has_next = jnp.logical_and(i >= 1, i < n_tiles - 1)
    return has_prev, has_next


def _tri_inverse(lmat):
    n = lmat.shape[0]
    eye = (lax.broadcasted_iota(jnp.int32, (n, n), 0) == lax.broadcasted_iota(jnp.int32, (n, n), 1))
    acc = jnp.where(eye, 1.0, 0.0) + lmat
    power = lmat
    for level in range(int(math.log2(CHUNK)) - 1):
        mm = _mm3 if level < INVERSE_HI_LEVELS else _mm1
        power = mm(power, power)
        acc = acc + mm(power, acc)
    return acc


def _icl_rate(xa, d, a0_ref, a1_ref, a2_ref):
    return jax.nn.sigmoid(a0_ref[d:d + 1, :] + _mm1(_mm1(xa, a1_ref[d]), a2_ref[d]))


def _rwkv_direction(d, r, k, v, kk, kd, aicl, logw, st):
    tm, dr = r.shape
    ri = lax.broadcasted_iota(jnp.int32, (tm, tm), 0)
    ci = lax.broadcasted_iota(jnp.int32, (tm, tm), 1)
    same_chunk = (ri // CHUNK) == (ci // CHUNK)
    before = (ci < ri) if d == 0 else (ci > ri)
    strict = jnp.logical_and(same_chunk, before)
    incl = jnp.logical_and(same_chunk, jnp.logical_or(before, ci == ri))
    cw = _mm_sel(jnp.where(incl, 1.0, 0.0).astype(BF16), logw)
    tot = _mm_sel(jnp.where(same_chunk, 1.0, 0.0).astype(BF16), logw)
    e_neg = jnp.exp(-cw)
    e_rem = jnp.exp(tot - cw)
    bvec = kk * aicl
    a_tb = (-kk * jnp.exp(cw - logw)).astype(BF16)
    r_t = r * jnp.exp(cw)
    r_tb = r_t.astype(BF16)
    v_b = v.astype(BF16)
    rhs_nt = jnp.concatenate([bvec * e_neg, kd * e_neg], 0).astype(BF16)
    lane = lax.broadcasted_iota(jnp.int32, (1, dr), 1)
    zero_b = jnp.zeros((), BF16)

    n_heads = dr // HEAD_DIM
    head_masks = [(lane // HEAD_DIM) == h for h in range(n_heads)]
    a_ms = [jnp.where(mh, a_tb, zero_b) for mh in head_masks]
    lhs = jnp.concatenate([x for h, mh in enumerate(head_masks) for x in (a_ms[h], jnp.where(mh, r_tb, zero_b))], 0)
    prod_all = _dot_nt(lhs, rhs_nt)

    pq_nat = jnp.zeros((tm, 2 * dr), F32)
    ry = r_t
    y0 = jnp.zeros((tm, dr), F32)
    for h, mh in enumerate(head_masks):
        prod = prod_all[h * 2 * tm:(h + 1) * 2 * tm]
        a_m = a_ms[h]
        v_m = jnp.where(mh, v_b, zero_b)
        lab = jnp.where(strict, prod[:tm, :tm], 0.0)
        lak = jnp.where(strict, prod[:tm, tm:], 0.0).astype(BF16)
        mr = jnp.where(jnp.concatenate([incl, incl], 1), prod[tm:, :], 0.0).astype(BF16)
        xv = _dot(jnp.concatenate([lak, mr[:, tm:]], 0), v_m)
        tinv = _tri_inverse(lab).astype(BF16)
        pq = _dot(tinv, jnp.concatenate([a_m, xv[:tm].astype(BF16)], 1))
        pq_nat = pq_nat + pq
        upd = _dot(mr[:, :tm], pq.astype(BF16))
        ry = ry + upd[:, :dr]
        y0 = y0 + (upd[:, dr:] + xv[tm:])

    p_b, q_b = pq_nat[:, :dr].astype(BF16), pq_nat[:, dr:].astype(BF16)
    bb_b, kb_b = (bvec * e_rem).astype(BF16), (kd * e_rem).astype(BF16)
    same_head = _group_ones(dr, HEAD_DIM)
    eye = lax.broadcasted_iota(jnp.int32, (dr, dr), 0) == lax.broadcasted_iota(jnp.int32, (dr, dr), 1)
    n_chunks = tm // CHUNK
    ys = [None] * n_chunks
    for c in (range(n_chunks) if d == 0 else reversed(range(n_chunks))):
        sl = slice(c * CHUNK, (c + 1) * CHUNK)
        gmat = _dot_tn(bb_b[sl], p_b[sl])
        hmat = _dot_tn(jnp.concatenate([bb_b[sl], kb_b[sl]], 0), jnp.concatenate([q_b[sl], v_b[sl]], 0))
        decay = jnp.exp(tot[c * CHUNK:c * CHUNK + 1, :])
        gmat = jnp.where(same_head, gmat, 0.0) + jnp.where(eye, decay, 0.0)
        ys[c] = _mm1(ry[sl], st) + y0[sl]
        st = _mm3(gmat, st) + jnp.where(same_head, hmat, 0.0)
    return jnp.concatenate(ys, 0), st


def _rwkv_scan_kernel(zf_ref, zfp_ref, zfn_ref, zr_ref, zrp_ref, zrn_ref, mu_ref, kk_ref, ka_ref, rk_ref,
                      w0_ref, w1_ref, w2_ref, a0_ref, a1_ref, a2_ref, g1_ref, g2_ref,
                      yf_ref, yr_ref, bonus_ref, gate_ref, st_ref, *, dr):
    i = pl.program_id(1)
    n_tiles = pl.num_programs(1)
    tm = zf_ref.shape[1]

    @pl.when(i == 0)
    def _():
        st_ref[...] = jnp.zeros(st_ref.shape, F32)

    head_ones = jnp.where(_group_ones(dr, HEAD_DIM), 1.0, 0.0).astype(BF16)
    mu = mu_ref[...]
    row = lax.broadcasted_iota(jnp.int32, (tm, 1), 0)
    tile_rev = jnp.where(i == 0, 0, n_tiles - i)
    for d, (z_ref, zp_ref, zn_ref, y_ref, tile) in enumerate(((zf_ref, zfp_ref, zfn_ref, yf_ref, i),
                                                              (zr_ref, zrp_ref, zrn_ref, yr_ref, tile_rev))):
        z = z_ref[0]
        has_prev, has_next = _segment_flags(tile, n_tiles)
        prow = jnp.where(has_prev, zp_ref[0, HALO - 1:HALO, :], 0.0)
        nrow = jnp.where(has_next, zn_ref[0, 0:1, :], 0.0)
        zprev = jnp.where(row == 0, prow, pltpu.roll(z, 1, 0))
        znext = jnp.where(row == tm - 1, nrow, pltpu.roll(z, tm - 1, 0))
        dz = 0.5 * (zprev + znext) - z
        r = z[:, 0:dr] + dz[:, 0:dr] * mu[0:1]
        k = z[:, dr:2 * dr] + dz[:, dr:2 * dr] * mu[1:2]
        v = z[:, 2 * dr:3 * dr] + dz[:, 2 * dr:3 * dr] * mu[2:3]
        zu, du = z[:, 3 * dr:4 * dr], dz[:, 3 * dr:4 * dr]
        xw = zu + du * mu[3:4]
        xa = zu + du * mu[4:5]
        kk = k * kk_ref[...]
        kk = kk / jnp.maximum(jnp.sqrt(_mm_sel_r(kk * kk, head_ones)), 1e-12)
        w_raw = w0_ref[d:d + 1, :] + _mm1(jnp.tanh(_mm1(xw, w1_ref[d])), w2_ref[d])
        logw = (-math.exp(-0.5)) * jax.nn.sigmoid(w_raw)
        aicl = _icl_rate(xa, d, a0_ref, a1_ref, a2_ref)
        kd = k * (1.0 + (aicl - 1.0) * ka_ref[...])
        if d == 0:
            xg = zu + du * mu[5:6]
            gate_ref[0] = _mm1(jax.nn.sigmoid(_mm1(xg, g1_ref[...])), g2_ref[...])
            kd_other = k * (1.0 + (_icl_rate(xa, 1, a0_ref, a1_ref, a2_ref) - 1.0) * ka_ref[...])
            bonus_ref[0] = _mm_sel_r(r * (kd + kd_other) * rk_ref[...], head_ones) * v
        y, st = _rwkv_direction(d, r, k, v, kk, kd, aicl, logw, st_ref[d])
        y_ref[0] = y
        st_ref[d] = st


def _rwkv(z, p, *, n_ctx):
    b, t, _ = z.shape
    dr = p["k_k"].shape[-1]
    tm = TOKEN_TILE
    nt = t // tm
    wz = 4 * dr
    if n_ctx != tm:
        raise ValueError("the context segment must be exactly one token tile")

    def tile_rev(i):
        return jnp.where(i == 0, 0, nt - i)

    fprev, fnext = _halo_specs(wz, 0, tm, nt)
    rprev, rnext = _halo_specs(wz, 0, tm, nt, tile_rev)

    def full(shape):
        return pl.BlockSpec(shape, lambda bb, i: (0,) * len(shape))

    out = jax.ShapeDtypeStruct((b, t, dr), F32)
    return pl.pallas_call(
        functools.partial(_rwkv_scan_kernel, dr=dr),
        grid=(b, nt),
        in_specs=[
            pl.BlockSpec((1, tm, wz), lambda bb, i: (bb, i, 0)), fprev, fnext,
            pl.BlockSpec((1, tm, wz), lambda bb, i: (bb, tile_rev(i), 0)), rprev, rnext,
            full((6, dr)), full((1, dr)), full((1, dr)), full((1, dr)),
            full((2, dr)), full(p["decay_w1"].shape), full(p["decay_w2"].shape),
            full((2, dr)), full(p["icl_a1"].shape), full(p["icl_a2"].shape),
            full(p["gate_g1"].shape), full(p["gate_g2"].shape),
        ],
        out_specs=[
            pl.BlockSpec((1, tm, dr), lambda bb, i: (bb, i, 0)),
            pl.BlockSpec((1, tm, dr), lambda bb, i: (bb, tile_rev(i), 0)),
            pl.BlockSpec((1, tm, dr), lambda bb, i: (bb, i, 0)),
            pl.BlockSpec((1, tm, dr), lambda bb, i: (bb, i, 0)),
        ],
        out_shape=[out, out, out, out],
        scratch_shapes=[pltpu.VMEM((2, dr, dr), F32)],
        compiler_params=_cparams(("arbitrary", "arbitrary")),
        name="rwkv_scan",
    )(z, z, z, z, z, z, p["rwkv_mu"], p["k_k"].reshape(1, dr), p["k_a"].reshape(1, dr), p["r_k"].reshape(1, dr),
      p["decay_w0"], p["decay_w1"], p["decay_w2"], p["icl_a0"], p["icl_a1"], p["icl_a2"],
      p["gate_g1"], p["gate_g2"])


def _pool_kernel(z_ref, zp_ref, zn_ref, w_ref, sc_ref, cs_ref, o_ref, ab_ref, *, n_ctx):
    i = pl.program_id(1)
    n_tiles = pl.num_programs(1)
    tm, dp = z_ref.shape[1], w_ref.shape[0]
    has_prev, has_next = _segment_flags(i, n_tiles)
    x = z_ref[0, :, :dp]
    ab_ref[0] = _mm3(z_ref[0, :, dp:], cs_ref[...]).astype(BF16)
    xe = jnp.concatenate([jnp.where(has_prev, zp_ref[0, :, :dp], 0.0), x,
                          jnp.where(has_next, zn_ref[0, :, :dp], 0.0)], 0)
    ne = tm + 2 * HALO

    def back(a, s):
        return pltpu.roll(a, s, 0)

    def ahead(a, s):
        return pltpu.roll(a, ne - s, 0)

    d2 = xe + back(xe, 1)
    d4 = d2 + back(d2, 2)
    d8 = d4 + back(d4, 4)
    e2 = xe + ahead(xe, 1)
    e4 = e2 + ahead(e2, 2)
    f1 = ahead(xe, 1)
    f3 = f1 + ahead(e2, 2)
    f7 = f3 + ahead(e4, 4)
    sums = (d2, d2 + back(xe, 2) + f1, d4 + back(xe, 4) + f3, d8 + back(xe, 8) + f7)

    tglob = i * tm + lax.broadcasted_iota(jnp.int32, (tm, 1), 0)
    n_lat = n_tiles * tm - n_ctx
    pos = jnp.where(i == 0, tglob, tglob - n_ctx)
    seg = jnp.where(i == 0, n_ctx, n_lat)
    lane = lax.broadcasted_iota(jnp.int32, (1, dp), 1)
    gw = dp // len(POOL_WINDOWS)
    pooled = jnp.zeros((tm, dp), F32)
    for g, win in enumerate(POOL_WINDOWS):
        nb, nf = win // 2, win - win // 2 - 1
        cnt = (jnp.minimum(pos, nb) + 1 + jnp.minimum(seg - 1 - pos, nf)).astype(F32)
        mean = sums[g][HALO:HALO + tm] / cnt
        pooled = jnp.where((lane // gw) == g, mean - x, pooled)
    o_ref[0] = _mm1(pooled, w_ref[...]) * sc_ref[...]


def _pool_chan_dft(z, w_bd, scale, chan_cs, *, n_ctx, col):
    b, t, _ = z.shape
    dp = w_bd.shape[0]
    df = chan_cs.shape[0]
    wz = dp + df
    tm = TOKEN_TILE
    if col % wz:
        raise ValueError("pool / Fourier columns must start on a multiple of their width")
    prev_spec, next_spec = _halo_specs(wz, col // wz, tm, t // tm)
    return pl.pallas_call(
        functools.partial(_pool_kernel, n_ctx=n_ctx),
        grid=(b, t // tm),
        in_specs=[
            pl.BlockSpec((1, tm, wz), lambda bb, i: (bb, i, col // wz)), prev_spec, next_spec,
            pl.BlockSpec((dp, dp), lambda bb, i: (0, 0)),
            pl.BlockSpec((1, dp), lambda bb, i: (0, 0)),
            pl.BlockSpec((df, 2 * df), lambda bb, i: (0, 0)),
        ],
        out_specs=[pl.BlockSpec((1, tm, dp), lambda bb, i: (bb, i, 0)),
                   pl.BlockSpec((1, tm, 2 * df), lambda bb, i: (bb, i, 0))],
        out_shape=[jax.ShapeDtypeStruct((b, t, dp), F32), jax.ShapeDtypeStruct((b, t, 2 * df), BF16)],
        compiler_params=_cparams(("arbitrary", "arbitrary")),
        name="pool_chan_dft",
    )(z, z, z, w_bd, scale.reshape(1, dp), chan_cs)


def _time_dft_kernel(ct_ref, st_ref, ab_ref, w_ref, o_ref, *, norm, row0):
    df = w_ref.shape[0]
    rows = ct_ref.shape[1]
    ab = ab_ref[0, row0:row0 + rows, :]
    f = (_dot(ct_ref[...], ab[:, :df]) + _dot(st_ref[...], ab[:, df:])) * norm
    o_ref[0] = _mm1(f, w_ref[...])


def _dft_tables(n):
    n2 = 1 << (int(math.log2(n)) // 2)
    n1 = n // n2
    if n1 * n2 != n:
        raise ValueError("DFT length must be a power of two")
    tp = jnp.arange(n, dtype=jnp.int32)[:, None]
    alpha = ((tp * jnp.arange(n1, dtype=jnp.int32)[None, :]) % n1).astype(F32) * (2.0 * math.pi / n1)
    beta = ((tp * jnp.arange(n2, dtype=jnp.int32)[None, :]) % n).astype(F32) * (2.0 * math.pi / n)
    ca, sa = jnp.cos(alpha)[:, :, None], jnp.sin(alpha)[:, :, None]
    cb, sb = jnp.cos(beta)[:, None, :], jnp.sin(beta)[:, None, :]
    return (ca * cb - sa * sb).reshape(n, n), (sa * cb + ca * sb).reshape(n, n)


def _fourier_segment(ab, tables, w, *, row0, n_chan_group):
    b, t, _ = ab.shape
    df = w.shape[0]
    ct, st = tables
    rows = ct.shape[0]
    tr = min(rows, 512)
    norm = 1.0 / math.sqrt(rows * n_chan_group)
    return pl.pallas_call(
        functools.partial(_time_dft_kernel, norm=norm, row0=row0),
        grid=(rows // tr, b),
        in_specs=[
            pl.BlockSpec((tr, rows), lambda i, bb: (i, 0)),
            pl.BlockSpec((tr, rows), lambda i, bb: (i, 0)),
            pl.BlockSpec((1, t, 2 * df), lambda i, bb: (bb, 0, 0)),
            pl.BlockSpec((df, df), lambda i, bb: (0, 0)),
        ],
        out_specs=pl.BlockSpec((1, tr, df), lambda i, bb: (bb, i, 0)),
        out_shape=jax.ShapeDtypeStruct((b, rows, df), F32),
        compiler_params=_cparams(("arbitrary", "arbitrary")),
        name=f"time_dft_{rows}",
    )(ct, st, ab, w)


def _fourier(ab, tables_ctx, tables_lat, w, *, n_ctx):
    four_c = _fourier_segment(ab, tables_ctx, w, row0=0, n_chan_group=HEAD_DIM)
    four_l = _fourier_segment(ab, tables_lat, w, row0=n_ctx, n_chan_group=HEAD_DIM)
    return four_c, four_l


def _out_ffn_kernel(x_ref, a_ref, yf_ref, yr_ref, bn_ref, gt_ref, p_ref, fc_ref, fl_ref, m_ref, gng_ref, gnb_ref,
                    wout_ref, g1_ref, b1_ref, wgu_ref, wdn_ref, g2_ref, b2_ref, o_ref, *, alpha, skip):
    x = x_ref[0]
    dr = yf_ref.shape[2]
    y = yf_ref[0] + yr_ref[0]
    head_ones = jnp.where(_group_ones(dr, HEAD_DIM), 1.0, 0.0).astype(BF16)
    mu = _mm_sel_r(y, head_ones) * (1.0 / HEAD_DIM)
    yc = y - mu
    var = _mm_sel_r(yc * yc, head_ones) * (1.0 / HEAD_DIM)
    rw = (yc * lax.rsqrt(var + GN_EPS) * gng_ref[...] + gnb_ref[...] + bn_ref[0]) * gt_ref[0]
    four = fl_ref[0] if skip else jnp.where(pl.program_id(1) == 0, fc_ref[0], fl_ref[0])
    gate = _mod_rows(m_ref, 1)[2]
    cat = jnp.concatenate([a_ref[0], rw, p_ref[0], four], -1).astype(BF16)
    x1 = _layer_norm(alpha * x + gate * _dot(cat, wout_ref[...]), g1_ref[...], b1_ref[...])
    o_ref[0] = _swiglu_postnorm(x1, _mod_rows(m_ref, 2), wgu_ref, wdn_ref, g2_ref[...], b2_ref[...], alpha)


def _out_ffn(xs, att, rwkv_parts, pool, four_c, four_l, mod_l, gn_g, gn_b, w_out, ln_g1, ln_b1, wgu, wdn,
             ln_g2, ln_b2, *, alpha, n_batch, latent_only):
    b, t, d = xs.shape
    tm = TOKEN_TILE
    dm = att.shape[-1]
    dff = wdn.shape[0]
    skip = 1 if latent_only else 0
    rows = lambda bb, i: (bb, i + skip, 0)
    mod_map = (lambda bb, i: (bb, 0, 0)) if latent_only else _mod_row_map(n_batch)
    branch = pl.BlockSpec((1, tm, dm), rows)
    return pl.pallas_call(
        functools.partial(_out_ffn_kernel, alpha=alpha, skip=skip),
        grid=(b, t // tm - skip),
        in_specs=[
            pl.BlockSpec((1, tm, d), rows), branch, branch, branch, branch, branch, branch,
            pl.BlockSpec((1, tm, dm), lambda bb, i: (bb, 0, 0)),
            pl.BlockSpec((1, tm, dm), lambda bb, i: (bb, jnp.maximum(i + skip - 1, 0), 0)),
            pl.BlockSpec((1, 9, d), mod_map),
            _resident((1, dm)), _resident((1, dm)),
            _resident((4 * dm, d)), _resident((1, d)), _resident((1, d)),
            _resident((d, 2 * dff)), _resident((dff, d)), _resident((1, d)), _resident((1, d)),
        ],
        out_specs=pl.BlockSpec((1, tm, d), lambda bb, i: (bb, i, 0)),
        out_shape=jax.ShapeDtypeStruct((b, t - skip * tm, d), F32),
        compiler_params=_cparams(("arbitrary", "arbitrary")),
        name="out_ffn",
    )(xs, att, *rwkv_parts, pool, four_c, four_l, mod_l, gn_g.reshape(1, dm), gn_b.reshape(1, dm), w_out,
      ln_g1.reshape(1, d), ln_b1.reshape(1, d), wgu, wdn, ln_g2.reshape(1, d), ln_b2.reshape(1, d))


def _rope_tables(n_ctx, n_lat, n_heads):
    pos = jnp.arange(n_lat)
    rows = (pos // GRID_W).astype(F32)
    cols = (pos % GRID_W).astype(F32)
    n_pair_axis = HEAD_DIM // 4
    inv = ROPE_THETA ** (-jnp.arange(n_pair_axis, dtype=F32) / n_pair_axis)
    ang = jnp.concatenate([rows[:, None] * inv, cols[:, None] * inv], -1)
    cos = jnp.repeat(jnp.cos(ang), 2, axis=-1)
    sin = jnp.stack([-jnp.sin(ang), jnp.sin(ang)], -1).reshape(n_lat, HEAD_DIM)
    cos = jnp.concatenate([jnp.ones((n_ctx, HEAD_DIM), F32), cos], 0)
    sin = jnp.concatenate([jnp.zeros((n_ctx, HEAD_DIM), F32), sin], 0)
    return jnp.tile(cos, (1, n_heads)), jnp.tile(sin, (1, n_heads))


def _block_diag(blocks):
    n, r, c = blocks.shape
    eye = jnp.eye(n, dtype=blocks.dtype)
    return (eye[:, None, :, None] * blocks[:, :, None, :]).reshape(n * r, n * c)


def kernel(x, c, ctx, c_ctx, w_mod, b_mod, ln_g, ln_b, w_ffn_in, w_ffn_out, w_in, q_norm_g, k_norm_g, rwkv_mu, decay_w0, decay_w1, decay_w2, icl_a0, icl_a1, icl_a2, gate_g1, gate_g2, k_k, k_a, r_k, gn_g, gn_b, pool_w, pool_scale, fourier_w, w_out):
    n_batch, n_lat, d = x.shape
    n_ctx = ctx.shape[1]
    depth = w_mod.shape[0]
    if n_ctx != TOKEN_TILE or n_lat % TOKEN_TILE or n_batch + 1 > 8:
        raise ValueError("unsupported shapes")
    alpha = (2 * depth) ** 0.25
    d_rwkv = k_k.shape[-1]
    d_pool = pool_scale.shape[-1]
    d_four = fourier_w.shape[-1]
    d_in = w_in.shape[-1]
    d_att = d_in - 4 * d_rwkv - d_pool - d_four
    n_q = (d_att - 2 * N_KV_HEADS * HEAD_DIM) // HEAD_DIM

    xs = jnp.concatenate([ctx, x], 1)
    cc = jnp.zeros((8, d), F32).at[:n_batch].set(c).at[n_batch].set(c_ctx)
    mod = _modulation(cc, w_mod, b_mod)

    cos_t, sin_t = _rope_tables(n_ctx, n_lat, n_q)
    qg = jnp.tile(q_norm_g, (1, n_q))
    kg = jnp.tile(k_norm_g, (1, n_q))
    cg, sg = _dft_tables(HEAD_DIM)
    n_fg = d_four // HEAD_DIM
    eye_g = jnp.eye(n_fg, dtype=F32)
    chan_cs = jnp.concatenate([jnp.kron(eye_g, cg), jnp.kron(eye_g, sg)], 1)
    ct_c, st_c = _dft_tables(n_ctx)
    ct_l, st_l = _dft_tables(n_lat)
    tables_ctx = (ct_c.astype(BF16), (-st_c).astype(BF16))
    tables_lat = (ct_l.astype(BF16), (-st_l).astype(BF16))
    w_in_r = jnp.concatenate([w_in[:, :, d_att:d_att + 4 * d_rwkv], w_in[:, :, :d_att],
                              w_in[:, :, d_att + 4 * d_rwkv:]], -1).astype(BF16)
    att_col = 4 * d_rwkv
    pool_col = att_col + d_att
    wgu = w_ffn_in.astype(BF16)
    wdn = w_ffn_out.astype(BF16)
    w_out_b = w_out.astype(BF16)

    for l in range(depth):
        last = l == depth - 1
        mod_l = mod[l].reshape(8, 9, d)
        xs, z = _ffn_in(xs, mod_l, wgu[l, 0], wdn[l, 0], ln_g[l, 0], ln_b[l, 0], w_in_r[l],
                        alpha=alpha, n_batch=n_batch)
        att = _attention(z, cos_t, sin_t, qg[l:l + 1], kg[l:l + 1], n_ctx=n_ctx, att_col=att_col)
        p = dict(rwkv_mu=rwkv_mu[l], decay_w0=decay_w0[l], decay_w1=decay_w1[l], decay_w2=decay_w2[l],
                 icl_a0=icl_a0[l], icl_a1=icl_a1[l], icl_a2=icl_a2[l], gate_g1=gate_g1[l], gate_g2=gate_g2[l],
                 k_k=k_k[l], k_a=k_a[l], r_k=r_k[l])
        rwkv_parts = _rwkv(z, p, n_ctx=n_ctx)
        pool, ab = _pool_chan_dft(z, _block_diag(pool_w[l]), pool_scale[l], chan_cs, n_ctx=n_ctx, col=pool_col)
        four_c, four_l = _fourier(ab, tables_ctx, tables_lat, fourier_w[l], n_ctx=n_ctx)
        xs = _out_ffn(xs, att, rwkv_parts, pool, four_c, four_l, mod_l, gn_g[l], gn_b[l], w_out_b[l],
                      ln_g[l, 1], ln_b[l, 1], wgu[l, 1], wdn[l, 1], ln_g[l, 2], ln_b[l, 2],
                      alpha=alpha, n_batch=n_batch, latent_only=last)
    return xs
```

```python
import functools
import math

import jax
import jax.numpy as jnp
from jax import lax
from jax.experimental import pallas as pl
from jax.experimental.pallas import tpu as pltpu

F32 = jnp.float32
BF16 = jnp.bfloat16

HEAD_DIM = 64
GRID_W = 64
POOL_WINDOWS = (2, 4, 8, 16)
ROPE_THETA = 10000.0
LN_EPS = 1e-5
QK_EPS = 1e-6
GN_EPS = 64e-5
N_KV_HEADS = 2

TOKEN_TILE = 256
CHUNK = 64
FFN_COL_BLOCK = 256
ATTN_ROW_SPLIT = 4
INVERSE_HI_LEVELS = 3
HALO = 8
VMEM_LIMIT = 56 * 1024 * 1024


def _cparams(sem, **kw):
    return pltpu.CompilerParams(dimension_semantics=sem, vmem_limit_bytes=VMEM_LIMIT, **kw)


def _dot(a, b):
    return jnp.dot(a, b, preferred_element_type=F32)


def _dot_nt(a, b):
    return lax.dot_general(a, b, (((1,), (1,)), ((), ())), preferred_element_type=F32)


def _dot_tn(a, b):
    return lax.dot_general(a, b, (((0,), (0,)), ((), ())), preferred_element_type=F32)


def _hi_lo(x):
    hi = x.astype(BF16)
    lo = (x - hi.astype(F32)).astype(BF16)
    return hi, lo


def _mm1(a, b, dot=_dot):
    return dot(a.astype(BF16), b.astype(BF16))


def _mm3(a, b):
    m = a.shape[0]
    a1, a2 = _hi_lo(a)
    b1, b2 = _hi_lo(b)
    top = _dot(jnp.concatenate([a1, a2], 0), b1)
    return top[:m] + (top[m:] + _dot(a1, b2))


def _mm_sel(sel_bf16, x):
    x1, x2 = _hi_lo(x)
    return _dot(sel_bf16, x1) + _dot(sel_bf16, x2)


def _mm_sel_r(x, sel_bf16):
    x1, x2 = _hi_lo(x)
    return _dot(x1, sel_bf16) + _dot(x2, sel_bf16)


def _group_ones(n, group):
    r = lax.broadcasted_iota(jnp.int32, (n, n), 0) // group
    c = lax.broadcasted_iota(jnp.int32, (n, n), 1) // group
    return r == c


def _layer_norm(t, g, b):
    mu = jnp.mean(t, -1, keepdims=True)
    tc = t - mu
    var = jnp.mean(tc * tc, -1, keepdims=True)
    return tc * lax.rsqrt(var + LN_EPS) * g + b


def _mod_kernel(c_ref, w_ref, b_ref, o_ref):
    c = c_ref[...]
    s = c * jax.nn.sigmoid(c)
    o_ref[0] = _mm1(s, w_ref[0]) + b_ref[0]


def _modulation(cc, w_mod, b_mod):
    depth, d, n = w_mod.shape
    tn = n // 4
    return pl.pallas_call(
        _mod_kernel,
        grid=(depth, n // tn),
        in_specs=[
            pl.BlockSpec((8, d), lambda l, j: (0, 0)),
            pl.BlockSpec((1, d, tn), lambda l, j: (l, 0, j)),
            pl.BlockSpec((1, 1, tn), lambda l, j: (l, 0, j)),
        ],
        out_specs=pl.BlockSpec((1, 8, tn), lambda l, j: (l, 0, j)),
        out_shape=jax.ShapeDtypeStruct((depth, 8, n), F32),
        compiler_params=_cparams(("arbitrary", "arbitrary")),
        name="modulation",
    )(cc, w_mod, b_mod.reshape(depth, 1, n))


def _mod_row_map(n_batch):
    return lambda b, i: (jnp.where(i == 0, n_batch, b), 0, 0)


def _mod_rows(m_ref, sub):
    return tuple(m_ref[0, 3 * sub + j:3 * sub + j + 1, :] for j in range(3))


def _swiglu_postnorm(x, mods, wgu_ref, wdn_ref, g, b, alpha):
    shift, scale, gate = mods
    dff = wdn_ref.shape[0]
    h = (x * (1.0 + scale) + shift).astype(BF16)
    acc = jnp.zeros(x.shape, F32)
    n = dff // FFN_COL_BLOCK
    gu, act = [None] * n, [None] * n
    for t in range(n + 2):
        if t < n:
            lo, hi = t * FFN_COL_BLOCK, (t + 1) * FFN_COL_BLOCK
            gu[t] = (_dot(h, wgu_ref[:, lo:hi]), _dot(h, wgu_ref[:, dff + lo:dff + hi]))
        if 0 <= t - 1 < n:
            gg, uu = gu[t - 1]
            act[t - 1] = (gg * jax.nn.sigmoid(gg) * uu).astype(BF16)
        if 0 <= t - 2 < n:
            lo, hi = (t - 2) * FFN_COL_BLOCK, (t - 1) * FFN_COL_BLOCK
            acc = acc + _dot(act[t - 2], wdn_ref[lo:hi, :])
    return _layer_norm(alpha * x + (0.5 * gate) * acc, g, b)


def _ffn_in_kernel(*refs, alpha, split_input):
    if split_input:
        xc_ref, xl_ref, m_ref, wgu_ref, wdn_ref, g_ref, b_ref, win_ref, o_ref, z_ref = refs
        x = jnp.where(pl.program_id(1) == 0, xc_ref[0], xl_ref[0])
    else:
        x_ref, m_ref, wgu_ref, wdn_ref, g_ref, b_ref, win_ref, o_ref, z_ref = refs
        x = x_ref[0]
    xo = _swiglu_postnorm(x, _mod_rows(m_ref, 0), wgu_ref, wdn_ref, g_ref[...], b_ref[...], alpha)
    o_ref[0] = xo
    shift, scale, _ = _mod_rows(m_ref, 1)
    z_ref[0] = _dot((xo * (1.0 + scale) + shift).astype(BF16), win_ref[...])


def _resident(shape):
    return pl.BlockSpec(shape, lambda bb, i: (0,) * len(shape), pipeline_mode=pl.Buffered(1))


def _ffn_in(xs, mod_l, wgu, wdn, ln_g, ln_b, w_in, *, alpha, n_batch):
    split_input = isinstance(xs, tuple)
    tm = TOKEN_TILE
    if split_input:
        b, t, d = xs[1].shape[0], xs[0].shape[1] + xs[1].shape[1], xs[1].shape[2]
        x_specs = [pl.BlockSpec((1, tm, d), lambda bb, i: (bb, 0, 0)),
                   pl.BlockSpec((1, tm, d), lambda bb, i: (bb, jnp.maximum(i - 1, 0), 0))]
    else:
        b, t, d = xs.shape
        xs = (xs,)
        x_specs = [pl.BlockSpec((1, tm, d), lambda bb, i: (bb, i, 0))]
    dff = wdn.shape[0]
    n = w_in.shape[1]
    if dff % FFN_COL_BLOCK:
        raise ValueError("FFN width must be a multiple of FFN_COL_BLOCK")
    return pl.pallas_call(
        functools.partial(_ffn_in_kernel, alpha=alpha, split_input=split_input),
        grid=(b, t // tm),
        in_specs=x_specs + [
            pl.BlockSpec((1, 9, d), _mod_row_map(n_batch)),
            _resident((d, 2 * dff)), _resident((dff, d)), _resident((1, d)), _resident((1, d)),
            _resident((d, n)),
        ],
        out_specs=[pl.BlockSpec((1, tm, d), lambda bb, i: (bb, i, 0)),
                   pl.BlockSpec((1, tm, n), lambda bb, i: (bb, i, 0))],
        out_shape=[jax.ShapeDtypeStruct((b, t, d), F32), jax.ShapeDtypeStruct((b, t, n), F32)],
        compiler_params=_cparams(("arbitrary", "arbitrary")),
        name="ffn_in",
    )(*xs, mod_l, wgu, wdn, ln_g.reshape(1, d), ln_b.reshape(1, d), w_in)


def _pair_swap(x):
    n = x.shape[-1]
    lane = lax.broadcasted_iota(jnp.int32, x.shape, 1)
    nxt = pltpu.roll(x, n - 1, 1)
    prv = pltpu.roll(x, 1, 1)
    return jnp.where(lane % 2 == 0, nxt, prv)


def _qkv_kernel(z_ref, cos_ref, sin_ref, qg_ref, kg_ref, q_ref, kt_ref, v_ref, *, n_q, n_kv):
    z = z_ref[0]
    dq, dk = n_q * HEAD_DIM, n_kv * HEAD_DIM
    zq, zk, zv = z[:, :dq], z[:, dq:dq + dk], z[:, dq + dk:dq + 2 * dk]
    cos, sin = cos_ref[...], sin_ref[...]
    xq = zq * qg_ref[...]
    yq = xq * cos + _pair_swap(xq) * sin
    xk = zk * kg_ref[...]
    yk = xk * cos[:, :dk] + _pair_swap(xk) * sin[:, :dk]
    q_scale = HEAD_DIM ** -0.5 * math.log2(math.e)
    for h in range(n_q):
        sl = slice(h * HEAD_DIM, (h + 1) * HEAD_DIM)
        ms = jnp.mean(zq[:, sl] * zq[:, sl], -1, keepdims=True)
        q_ref[0, h] = (yq[:, sl] * (lax.rsqrt(ms + QK_EPS) * q_scale)).astype(BF16)
    lane = lax.broadcasted_iota(jnp.int32, (1, dk), 1)
    inv = jnp.zeros(zk.shape, F32)
    for h in range(n_kv):
        sl = slice(h * HEAD_DIM, (h + 1) * HEAD_DIM)
        in_head = (lane // HEAD_DIM) == h
        ms = jnp.mean(zk[:, sl] * zk[:, sl], -1, keepdims=True)
        inv = jnp.where(in_head, lax.rsqrt(ms + QK_EPS), inv)
        v_ref[0, h] = jnp.where(in_head, zv, 1.0).astype(BF16)
    kt_ref[0] = (yk * inv).T.astype(BF16)


def _attn_kernel(q_ref, kt_ref, v_ref, o_ref, *, n_ctx, group):
    hkv = pl.program_id(1)
    i = pl.program_id(2)
    tq = q_ref.shape[2]
    q = q_ref[0].reshape(group * tq, HEAD_DIM)

    def run(kt, vv):
        n = ATTN_ROW_SPLIT
        rb = group * tq // n
        s, p, parts = [None] * n, [None] * n, [None] * n
        for t in range(n + 2):
            if t < n:
                s[t] = _dot(q[t * rb:(t + 1) * rb], kt)
            if 0 <= t - 1 < n:
                p[t - 1] = jnp.exp2(s[t - 1] - jnp.max(s[t - 1], -1, keepdims=True)).astype(BF16)
            if 0 <= t - 2 < n:
                parts[t - 2] = _dot(p[t - 2], vv)
        o = jnp.concatenate(parts, 0)
        r = o / pltpu.roll(o, HEAD_DIM, 1)
        rs = pltpu.roll(r, HEAD_DIM, 1)
        lane = lax.broadcasted_iota(jnp.int32, (1, 2 * HEAD_DIM), 1)
        first = hkv % 2 == 0
        low = jnp.where(first, r[:tq], rs[:tq])
        high = jnp.where(first, rs[tq:], r[tq:])
        o_ref[0] = jnp.where(lane < HEAD_DIM, low, high)

    @pl.when(i == 0)
    def _():
        run(kt_ref[0, :, :n_ctx], v_ref[0, 0, :n_ctx])

    @pl.when(i > 0)
    def _():
        run(kt_ref[0], v_ref[0, 0])


def _attention(z, cos_t, sin_t, qg, kg, *, n_ctx, att_col):
    b, t, _ = z.shape
    tm = TOKEN_TILE
    n_q = qg.shape[1] // HEAD_DIM
    n_kv = N_KV_HEADS
    group = n_q // n_kv
    if group != 2 or n_kv != 2:
        raise ValueError("attention kernel is written for 2 kv heads x 2 query heads")
    wq = (n_q + 2 * n_kv) * HEAD_DIM
    q, kt, v = pl.pallas_call(
        functools.partial(_qkv_kernel, n_q=n_q, n_kv=n_kv),
        grid=(b, t // tm),
        in_specs=[
            pl.BlockSpec((1, tm, wq), lambda bb, i: (bb, i, att_col // wq)),
            pl.BlockSpec((tm, n_q * HEAD_DIM), lambda bb, i: (i, 0)),
            pl.BlockSpec((tm, n_q * HEAD_DIM), lambda bb, i: (i, 0)),
            pl.BlockSpec((1, n_q * HEAD_DIM), lambda bb, i: (0, 0)),
            pl.BlockSpec((1, n_kv * HEAD_DIM), lambda bb, i: (0, 0)),
        ],
        out_specs=[
            pl.BlockSpec((1, n_q, tm, HEAD_DIM), lambda bb, i: (bb, 0, i, 0)),
            pl.BlockSpec((1, n_kv * HEAD_DIM, tm), lambda bb, i: (bb, 0, i)),
            pl.BlockSpec((1, n_kv, tm, n_kv * HEAD_DIM), lambda bb, i: (bb, 0, i, 0)),
        ],
        out_shape=[
            jax.ShapeDtypeStruct((b, n_q, t, HEAD_DIM), BF16),
            jax.ShapeDtypeStruct((b, n_kv * HEAD_DIM, t), BF16),
            jax.ShapeDtypeStruct((b, n_kv, t, n_kv * HEAD_DIM), BF16),
        ],
        compiler_params=_cparams(("arbitrary", "arbitrary")),
        name="qkv_prep",
    )(z, cos_t, sin_t, qg, kg[:, :n_kv * HEAD_DIM])
    return pl.pallas_call(
        functools.partial(_attn_kernel, n_ctx=n_ctx, group=group),
        grid=(b, n_kv, t // tm),
        in_specs=[
            pl.BlockSpec((1, group, tm, HEAD_DIM), lambda bb, h, i: (bb, h, i, 0)),
            pl.BlockSpec((1, HEAD_DIM, t), lambda bb, h, i: (bb, h, 0)),
            pl.BlockSpec((1, 1, t, n_kv * HEAD_DIM), lambda bb, h, i: (bb, h, 0, 0)),
        ],
        out_specs=pl.BlockSpec((1, tm, group * HEAD_DIM), lambda bb, h, i: (bb, i, h)),
        out_shape=jax.ShapeDtypeStruct((b, t, n_q * HEAD_DIM), F32),
        compiler_params=_cparams(("arbitrary", "arbitrary", "arbitrary")),
        name="attention",
    )(q, kt, v)


def _halo_specs(width, col_block, rows_per_tile, n_tiles, tile_of=lambda i: i):
    r = rows_per_tile // HALO

    def prev_map(bb, i, *_):
        return (bb, jnp.maximum(tile_of(i) * r - 1, 0), col_block)

    def next_map(bb, i, *_):
        return (bb, jnp.minimum((tile_of(i) + 1) * r, n_tiles * r - 1), col_block)

    return (pl.BlockSpec((1, HALO, width), prev_map), pl.BlockSpec((1, HALO, width), next_map))


def _segment_flags(i, n_tiles):
    has_prev = i >= 2
    has_next = jnp.logical_and(i >= 1, i < n_tiles - 1)
    return has_prev, has_next


def _tri_inverse_many(lmats):
    n = lmats[0].shape[0]
    eye = (lax.broadcasted_iota(jnp.int32, (n, n), 0) == lax.broadcasted_iota(jnp.int32, (n, n), 1))
    accs = [jnp.where(eye, 1.0, 0.0) + lm for lm in lmats]
    powers = list(lmats)
    for level in range(int(math.log2(CHUNK)) - 1):
        mm = _mm3 if level < INVERSE_HI_LEVELS else _mm1
        powers = [mm(p, p) for p in powers]
        accs = [a + mm(p, a) for p, a in zip(powers, accs)]
    return accs


def _icl_rate(xa, d, a0_ref, a1_ref, a2_ref):
    return jax.nn.sigmoid(a0_ref[d:d + 1, :] + _mm1(_mm1(xa, a1_ref[d]), a2_ref[d]))


def _chunk_setup(d, r, v, kk, kd, aicl, logw):
    tm, dr = r.shape
    ri = lax.broadcasted_iota(jnp.int32, (tm, tm), 0)
    ci = lax.broadcasted_iota(jnp.int32, (tm, tm), 1)
    same_chunk = (ri // CHUNK) == (ci // CHUNK)
    before = (ci < ri) if d == 0 else (ci > ri)
    strict = jnp.logical_and(same_chunk, before)
    incl = jnp.logical_and(same_chunk, jnp.logical_or(before, ci == ri))
    cw = _mm_sel(jnp.where(incl, 1.0, 0.0).astype(BF16), logw)
    tot = _mm_sel(jnp.where(same_chunk, 1.0, 0.0).astype(BF16), logw)
    e_neg = jnp.exp(-cw)
    e_rem = jnp.exp(tot - cw)
    bvec = kk * aicl
    a_tb = (-kk * jnp.exp(cw - logw)).astype(BF16)
    r_t = r * jnp.exp(cw)
    r_tb = r_t.astype(BF16)
    v_b = v.astype(BF16)
    rhs_nt = jnp.concatenate([bvec * e_neg, kd * e_neg], 0).astype(BF16)
    lane = lax.broadcasted_iota(jnp.int32, (1, dr), 1)
    zero_b = jnp.zeros((), BF16)

    head_masks = [(lane // HEAD_DIM) == h for h in range(dr // HEAD_DIM)]
    a_ms = [jnp.where(mh, a_tb, zero_b) for mh in head_masks]
    lhs = jnp.concatenate([x for h, mh in enumerate(head_masks) for x in (a_ms[h], jnp.where(mh, r_tb, zero_b))], 0)
    prod_all = _dot_nt(lhs, rhs_nt)
    incl2 = jnp.concatenate([incl, incl], 1)
    labs, mrs, xvs = [], [], []
    for h, mh in enumerate(head_masks):
        prod = prod_all[h * 2 * tm:(h + 1) * 2 * tm]
        labs.append(jnp.where(strict, prod[:tm, :tm], 0.0))
        lak = jnp.where(strict, prod[:tm, tm:], 0.0).astype(BF16)
        mr = jnp.where(incl2, prod[tm:, :], 0.0).astype(BF16)
        mrs.append(mr)
        xvs.append(_dot(jnp.concatenate([lak, mr[:, tm:]], 0), jnp.where(mh, v_b, zero_b)))
    return dict(d=d, labs=labs, mrs=mrs, xvs=xvs, a_ms=a_ms, r_t=r_t, v_b=v_b, tot=tot,
                bb_b=(bvec * e_rem).astype(BF16), kb_b=(kd * e_rem).astype(BF16))


def _chunk_finish(setups, tinvs, states):
    tm, dr = setups[0]["r_t"].shape
    n_heads = dr // HEAD_DIM
    n_chunks = tm // CHUNK
    pqs = [[_dot(tinvs[j][h].astype(BF16), jnp.concatenate([s["a_ms"][h], s["xvs"][h][:tm].astype(BF16)], 1))
            for h in range(n_heads)] for j, s in enumerate(setups)]
    rys, y0s, p_bs, q_bs = [], [], [], []
    for j, s in enumerate(setups):
        pq_nat = jnp.zeros((tm, 2 * dr), F32)
        ry = s["r_t"]
        y0 = jnp.zeros((tm, dr), F32)
        for h in range(n_heads):
            pq_nat = pq_nat + pqs[j][h]
            upd = _dot(s["mrs"][h][:, :tm], pqs[j][h].astype(BF16))
            ry = ry + upd[:, :dr]
            y0 = y0 + (upd[:, dr:] + s["xvs"][h][tm:])
        rys.append(ry)
        y0s.append(y0)
        p_bs.append(pq_nat[:, :dr].astype(BF16))
        q_bs.append(pq_nat[:, dr:].astype(BF16))

    same_head = _group_ones(dr, HEAD_DIM)
    eye = lax.broadcasted_iota(jnp.int32, (dr, dr), 0) == lax.broadcasted_iota(jnp.int32, (dr, dr), 1)
    states = list(states)
    ys = [[None] * n_chunks for _ in setups]
    for step in range(n_chunks):
        for j, s in enumerate(setups):
            c = step if s["d"] == 0 else n_chunks - 1 - step
            sl = slice(c * CHUNK, (c + 1) * CHUNK)
            gmat = _dot_tn(s["bb_b"][sl], p_bs[j][sl])
            hmat = _dot_tn(jnp.concatenate([s["bb_b"][sl], s["kb_b"][sl]], 0),
                           jnp.concatenate([q_bs[j][sl], s["v_b"][sl]], 0))
            decay = jnp.exp(s["tot"][c * CHUNK:c * CHUNK + 1, :])
            gmat = jnp.where(same_head, gmat, 0.0) + jnp.where(eye, decay, 0.0)
            ys[j][c] = _mm1(rys[j][sl], states[j]) + y0s[j][sl]
            states[j] = _mm3(gmat, states[j]) + jnp.where(same_head, hmat, 0.0)
    return [jnp.concatenate(y, 0) for y in ys], states


def _rwkv_scan_kernel(zf_ref, zfp_ref, zfn_ref, zr_ref, zrp_ref, zrn_ref, mu_ref, kk_ref, ka_ref, rk_ref,
                      w0_ref, w1_ref, w2_ref, a0_ref, a1_ref, a2_ref, g1_ref, g2_ref,
                      yf_ref, yr_ref, bonus_ref, gate_ref, st_ref, *, dr):
    i = pl.program_id(1)
    n_tiles = pl.num_programs(1)
    tm = zf_ref.shape[1]

    @pl.when(i == 0)
    def _():
        st_ref[...] = jnp.zeros(st_ref.shape, F32)

    head_ones = jnp.where(_group_ones(dr, HEAD_DIM), 1.0, 0.0).astype(BF16)
    mu = mu_ref[...]
    row = lax.broadcasted_iota(jnp.int32, (tm, 1), 0)
    tile_rev = jnp.where(i == 0, 0, n_tiles - i)
    setups = []
    for d, (z_ref, zp_ref, zn_ref, tile) in enumerate(((zf_ref, zfp_ref, zfn_ref, i), (zr_ref, zrp_ref, zrn_ref, tile_rev))):
        z = z_ref[0]
        has_prev, has_next = _segment_flags(tile, n_tiles)
        prow = jnp.where(has_prev, zp_ref[0, HALO - 1:HALO, :], 0.0)
        nrow = jnp.where(has_next, zn_ref[0, 0:1, :], 0.0)
        zprev = jnp.where(row == 0, prow, pltpu.roll(z, 1, 0))
        znext = jnp.where(row == tm - 1, nrow, pltpu.roll(z, tm - 1, 0))
        dz = 0.5 * (zprev + znext) - z
        r = z[:, 0:dr] + dz[:, 0:dr] * mu[0:1]
        k = z[:, dr:2 * dr] + dz[:, dr:2 * dr] * mu[1:2]
        v = z[:, 2 * dr:3 * dr] + dz[:, 2 * dr:3 * dr] * mu[2:3]
        zu, du = z[:, 3 * dr:4 * dr], dz[:, 3 * dr:4 * dr]
        xw = zu + du * mu[3:4]
        xa = zu + du * mu[4:5]
        kk = k * kk_ref[...]
        kk = kk / jnp.maximum(jnp.sqrt(_mm_sel_r(kk * kk, head_ones)), 1e-12)
        w_raw = w0_ref[d:d + 1, :] + _mm1(jnp.tanh(_mm1(xw, w1_ref[d])), w2_ref[d])
        logw = (-math.exp(-0.5)) * jax.nn.sigmoid(w_raw)
        aicl = _icl_rate(xa, d, a0_ref, a1_ref, a2_ref)
        kd = k * (1.0 + (aicl - 1.0) * ka_ref[...])
        if d == 0:
            xg = zu + du * mu[5:6]
            gate_ref[0] = _mm1(jax.nn.sigmoid(_mm1(xg, g1_ref[...])), g2_ref[...])
            kd_other = k * (1.0 + (_icl_rate(xa, 1, a0_ref, a1_ref, a2_ref) - 1.0) * ka_ref[...])
            bonus_ref[0] = _mm_sel_r(r * (kd + kd_other) * rk_ref[...], head_ones) * v
        setups.append(_chunk_setup(d, r, v, kk, kd, aicl, logw))
    n_heads = dr // HEAD_DIM
    flat = _tri_inverse_many([lab for s in setups for lab in s["labs"]])
    tinvs = [flat[j * n_heads:(j + 1) * n_heads] for j in range(len(setups))]
    ys, states = _chunk_finish(setups, tinvs, [st_ref[0], st_ref[1]])
    yf_ref[0] = ys[0]
    yr_ref[0] = ys[1]
    st_ref[0] = states[0]
    st_ref[1] = states[1]


def _rwkv(z, p, *, n_ctx):
    b, t, _ = z.shape
    dr = p["k_k"].shape[-1]
    tm = TOKEN_TILE
    nt = t // tm
    wz = 4 * dr
    if n_ctx != tm:
        raise ValueError("the context segment must be exactly one token tile")

    def tile_rev(i):
        return jnp.where(i == 0, 0, nt - i)

    fprev, fnext = _halo_specs(wz, 0, tm, nt)
    rprev, rnext = _halo_specs(wz, 0, tm, nt, tile_rev)

    def full(shape):
        return pl.BlockSpec(shape, lambda bb, i: (0,) * len(shape))

    out = jax.ShapeDtypeStruct((b, t, dr), F32)
    return pl.pallas_call(
        functools.partial(_rwkv_scan_kernel, dr=dr),
        grid=(b, nt),
        in_specs=[
            pl.BlockSpec((1, tm, wz), lambda bb, i: (bb, i, 0)), fprev, fnext,
            pl.BlockSpec((1, tm, wz), lambda bb, i: (bb, tile_rev(i), 0)), rprev, rnext,
            full((6, dr)), full((1, dr)), full((1, dr)), full((1, dr)),
            full((2, dr)), full(p["decay_w1"].shape), full(p["decay_w2"].shape),
            full((2, dr)), full(p["icl_a1"].shape), full(p["icl_a2"].shape),
            full(p["gate_g1"].shape), full(p["gate_g2"].shape),
        ],
        out_specs=[
            pl.BlockSpec((1, tm, dr), lambda bb, i: (bb, i, 0)),
            pl.BlockSpec((1, tm, dr), lambda bb, i: (bb, tile_rev(i), 0)),
            pl.BlockSpec((1, tm, dr), lambda bb, i: (bb, i, 0)),
            pl.BlockSpec((1, tm, dr), lambda bb, i: (bb, i, 0)),
        ],
        out_shape=[out, out, out, out],
        scratch_shapes=[pltpu.VMEM((2, dr, dr), F32)],
        compiler_params=_cparams(("arbitrary", "arbitrary")),
        name="rwkv_scan",
    )(z, z, z, z, z, z, p["rwkv_mu"], p["k_k"].reshape(1, dr), p["k_a"].reshape(1, dr), p["r_k"].reshape(1, dr),
      p["decay_w0"], p["decay_w1"], p["decay_w2"], p["icl_a0"], p["icl_a1"], p["icl_a2"],
      p["gate_g1"], p["gate_g2"])


def _pool_kernel(z_ref, zp_ref, zn_ref, w_ref, sc_ref, cs_ref, o_ref, ab_ref, *, n_ctx):
    i = pl.program_id(1)
    n_tiles = pl.num_programs(1)
    tm, dp = z_ref.shape[1], w_ref.shape[0]
    has_prev, has_next = _segment_flags(i, n_tiles)
    x = z_ref[0, :, :dp]
    ab_ref[0] = _mm3(z_ref[0, :, dp:], cs_ref[...]).astype(BF16)
    xe = jnp.concatenate([jnp.where(has_prev, zp_ref[0, :, :dp], 0.0), x,
                          jnp.where(has_next, zn_ref[0, :, :dp], 0.0)], 0)
    ne = tm + 2 * HALO

    def back(a, s):
        return pltpu.roll(a, s, 0)

    def ahead(a, s):
        return pltpu.roll(a, ne - s, 0)

    d2 = xe + back(xe, 1)
    d4 = d2 + back(d2, 2)
    d8 = d4 + back(d4, 4)
    e2 = xe + ahead(xe, 1)
    e4 = e2 + ahead(e2, 2)
    f1 = ahead(xe, 1)
    f3 = f1 + ahead(e2, 2)
    f7 = f3 + ahead(e4, 4)
    sums = (d2, d2 + back(xe, 2) + f1, d4 + back(xe, 4) + f3, d8 + back(xe, 8) + f7)

    tglob = i * tm + lax.broadcasted_iota(jnp.int32, (tm, 1), 0)
    n_lat = n_tiles * tm - n_ctx
    pos = jnp.where(i == 0, tglob, tglob - n_ctx)
    seg = jnp.where(i == 0, n_ctx, n_lat)
    lane = lax.broadcasted_iota(jnp.int32, (1, dp), 1)
    gw = dp // len(POOL_WINDOWS)
    pooled = jnp.zeros((tm, dp), F32)
    for g, win in enumerate(POOL_WINDOWS):
        nb, nf = win // 2, win - win // 2 - 1
        cnt = (jnp.minimum(pos, nb) + 1 + jnp.minimum(seg - 1 - pos, nf)).astype(F32)
        mean = sums[g][HALO:HALO + tm] / cnt
        pooled = jnp.where((lane // gw) == g, mean - x, pooled)
    o_ref[0] = _mm1(pooled, w_ref[...]) * sc_ref[...]


def _pool_chan_dft(z, w_bd, scale, chan_cs, *, n_ctx, col):
    b, t, _ = z.shape
    dp = w_bd.shape[0]
    df = chan_cs.shape[0]
    wz = dp + df
    tm = TOKEN_TILE
    if col % wz:
        raise ValueError("pool / Fourier columns must start on a multiple of their width")
    prev_spec, next_spec = _halo_specs(wz, col // wz, tm, t // tm)
    return pl.pallas_call(
        functools.partial(_pool_kernel, n_ctx=n_ctx),
        grid=(b, t // tm),
        in_specs=[
            pl.BlockSpec((1, tm, wz), lambda bb, i: (bb, i, col // wz)), prev_spec, next_spec,
            pl.BlockSpec((dp, dp), lambda bb, i: (0, 0)),
            pl.BlockSpec((1, dp), lambda bb, i: (0, 0)),
            pl.BlockSpec((df, 2 * df), lambda bb, i: (0, 0)),
        ],
        out_specs=[pl.BlockSpec((1, tm, dp), lambda bb, i: (bb, i, 0)),
                   pl.BlockSpec((1, tm, 2 * df), lambda bb, i: (bb, i, 0))],
        out_shape=[jax.ShapeDtypeStruct((b, t, dp), F32), jax.ShapeDtypeStruct((b, t, 2 * df), BF16)],
        compiler_params=_cparams(("arbitrary", "arbitrary")),
        name="pool_chan_dft",
    )(z, z, z, w_bd, scale.reshape(1, dp), chan_cs)


def _time_dft_kernel(ct_ref, st_ref, ab_ref, w_ref, o_ref, *, norm, row0):
    df = w_ref.shape[0]
    rows = ct_ref.shape[1]
    ab = ab_ref[0, row0:row0 + rows, :]
    f = (_dot(ct_ref[...], ab[:, :df]) + _dot(st_ref[...], ab[:, df:])) * norm
    o_ref[0] = _mm1(f, w_ref[...])


def _dft_tables(n):
    n2 = 1 << (int(math.log2(n)) // 2)
    n1 = n // n2
    if n1 * n2 != n:
        raise ValueError("DFT length must be a power of two")
    tp = jnp.arange(n, dtype=jnp.int32)[:, None]
    alpha = ((tp * jnp.arange(n1, dtype=jnp.int32)[None, :]) % n1).astype(F32) * (2.0 * math.pi / n1)
    beta = ((tp * jnp.arange(n2, dtype=jnp.int32)[None, :]) % n).astype(F32) * (2.0 * math.pi / n)
    ca, sa = jnp.cos(alpha)[:, :, None], jnp.sin(alpha)[:, :, None]
    cb, sb = jnp.cos(beta)[:, None, :], jnp.sin(beta)[:, None, :]
    return (ca * cb - sa * sb).reshape(n, n), (sa * cb + ca * sb).reshape(n, n)


def _fourier_segment(ab, tables, w, *, row0, n_chan_group):
    b, t, _ = ab.shape
    df = w.shape[0]
    ct, st = tables
    rows = ct.shape[0]
    tr = min(rows, 512)
    norm = 1.0 / math.sqrt(rows * n_chan_group)
    return pl.pallas_call(
        functools.partial(_time_dft_kernel, norm=norm, row0=row0),
        grid=(rows // tr, b),
        in_specs=[
            pl.BlockSpec((tr, rows), lambda i, bb: (i, 0)),
            pl.BlockSpec((tr, rows), lambda i, bb: (i, 0)),
            pl.BlockSpec((1, t, 2 * df), lambda i, bb: (bb, 0, 0)),
            pl.BlockSpec((df, df), lambda i, bb: (0, 0)),
        ],
        out_specs=pl.BlockSpec((1, tr, df), lambda i, bb: (bb, i, 0)),
        out_shape=jax.ShapeDtypeStruct((b, rows, df), F32),
        compiler_params=_cparams(("arbitrary", "arbitrary")),
        name=f"time_dft_{rows}",
    )(ct, st, ab, w)


def _fourier(ab, tables_ctx, tables_lat, w, *, n_ctx):
    four_c = _fourier_segment(ab, tables_ctx, w, row0=0, n_chan_group=HEAD_DIM)
    four_l = _fourier_segment(ab, tables_lat, w, row0=n_ctx, n_chan_group=HEAD_DIM)
    return four_c, four_l


def _out_ffn_kernel(x_ref, a_ref, yf_ref, yr_ref, bn_ref, gt_ref, p_ref, fc_ref, fl_ref, m_ref, gng_ref, gnb_ref,
                    wout_ref, g1_ref, b1_ref, wgu_ref, wdn_ref, g2_ref, b2_ref, o_ref, *, alpha, skip):
    x = x_ref[0]
    dr = yf_ref.shape[2]
    y = yf_ref[0] + yr_ref[0]
    head_ones = jnp.where(_group_ones(dr, HEAD_DIM), 1.0, 0.0).astype(BF16)
    mu = _mm_sel_r(y, head_ones) * (1.0 / HEAD_DIM)
    yc = y - mu
    var = _mm_sel_r(yc * yc, head_ones) * (1.0 / HEAD_DIM)
    rw = (yc * lax.rsqrt(var + GN_EPS) * gng_ref[...] + gnb_ref[...] + bn_ref[0]) * gt_ref[0]
    four = fl_ref[0] if skip else jnp.where(pl.program_id(1) == 0, fc_ref[0], fl_ref[0])
    gate = _mod_rows(m_ref, 1)[2]
    cat = jnp.concatenate([a_ref[0], rw, p_ref[0], four], -1).astype(BF16)
    x1 = _layer_norm(alpha * x + gate * _dot(cat, wout_ref[...]), g1_ref[...], b1_ref[...])
    o_ref[0] = _swiglu_postnorm(x1, _mod_rows(m_ref, 2), wgu_ref, wdn_ref, g2_ref[...], b2_ref[...], alpha)


def _out_ffn(xs, att, rwkv_parts, pool, four_c, four_l, mod_l, gn_g, gn_b, w_out, ln_g1, ln_b1, wgu, wdn,
             ln_g2, ln_b2, *, alpha, n_batch, latent_only):
    b, t, d = xs.shape
    tm = TOKEN_TILE
    dm = att.shape[-1]
    dff = wdn.shape[0]
    skip = 1 if latent_only else 0
    rows = lambda bb, i: (bb, i + skip, 0)
    mod_map = (lambda bb, i: (bb, 0, 0)) if latent_only else _mod_row_map(n_batch)
    branch = pl.BlockSpec((1, tm, dm), rows)
    return pl.pallas_call(
        functools.partial(_out_ffn_kernel, alpha=alpha, skip=skip),
        grid=(b, t // tm - skip),
        in_specs=[
            pl.BlockSpec((1, tm, d), rows), branch, branch, branch, branch, branch, branch,
            pl.BlockSpec((1, tm, dm), lambda bb, i: (bb, 0, 0)),
            pl.BlockSpec((1, tm, dm), lambda bb, i: (bb, jnp.maximum(i + skip - 1, 0), 0)),
            pl.BlockSpec((1, 9, d), mod_map),
            _resident((1, dm)), _resident((1, dm)),
            _resident((4 * dm, d)), _resident((1, d)), _resident((1, d)),
            _resident((d, 2 * dff)), _resident((dff, d)), _resident((1, d)), _resident((1, d)),
        ],
        out_specs=pl.BlockSpec((1, tm, d), lambda bb, i: (bb, i, 0)),
        out_shape=jax.ShapeDtypeStruct((b, t - skip * tm, d), F32),
        compiler_params=_cparams(("arbitrary", "arbitrary")),
        name="out_ffn",
    )(xs, att, *rwkv_parts, pool, four_c, four_l, mod_l, gn_g.reshape(1, dm), gn_b.reshape(1, dm), w_out,
      ln_g1.reshape(1, d), ln_b1.reshape(1, d), wgu, wdn, ln_g2.reshape(1, d), ln_b2.reshape(1, d))


def _rope_tables(n_ctx, n_lat, n_heads):
    pos = jnp.arange(n_lat)
    rows = (pos // GRID_W).astype(F32)
    cols = (pos % GRID_W).astype(F32)
    n_pair_axis = HEAD_DIM // 4
    inv = ROPE_THETA ** (-jnp.arange(n_pair_axis, dtype=F32) / n_pair_axis)
    ang = jnp.concatenate([rows[:, None] * inv, cols[:, None] * inv], -1)
    cos = jnp.repeat(jnp.cos(ang), 2, axis=-1)
    sin = jnp.stack([-jnp.sin(ang), jnp.sin(ang)], -1).reshape(n_lat, HEAD_DIM)
    cos = jnp.concatenate([jnp.ones((n_ctx, HEAD_DIM), F32), cos], 0)
    sin = jnp.concatenate([jnp.zeros((n_ctx, HEAD_DIM), F32), sin], 0)
    return jnp.tile(cos, (1, n_heads)), jnp.tile(sin, (1, n_heads))


def _block_diag(blocks):
    n, r, c = blocks.shape
    eye = jnp.eye(n, dtype=blocks.dtype)
    return (eye[:, None, :, None] * blocks[:, :, None, :]).reshape(n * r, n * c)


def kernel(x, c, ctx, c_ctx, w_mod, b_mod, ln_g, ln_b, w_ffn_in, w_ffn_out, w_in, q_norm_g, k_norm_g, rwkv_mu, decay_w0, decay_w1, decay_w2, icl_a0, icl_a1, icl_a2, gate_g1, gate_g2, k_k, k_a, r_k, gn_g, gn_b, pool_w, pool_scale, fourier_w, w_out):
    n_batch, n_lat, d = x.shape
    n_ctx = ctx.shape[1]
    depth = w_mod.shape[0]
    if n_ctx != TOKEN_TILE or n_lat % TOKEN_TILE or n_batch + 1 > 8:
        raise ValueError("unsupported shapes")
    alpha = (2 * depth) ** 0.25
    d_rwkv = k_k.shape[-1]
    d_pool = pool_scale.shape[-1]
    d_four = fourier_w.shape[-1]
    d_in = w_in.shape[-1]
    d_att = d_in - 4 * d_rwkv - d_pool - d_four
    n_q = (d_att - 2 * N_KV_HEADS * HEAD_DIM) // HEAD_DIM

    xs = (ctx, x)
    cc = jnp.zeros((8, d), F32).at[:n_batch].set(c).at[n_batch].set(c_ctx)
    mod = _modulation(cc, w_mod, b_mod)

    cos_t, sin_t = _rope_tables(n_ctx, n_lat, n_q)
    qg = jnp.tile(q_norm_g, (1, n_q))
    kg = jnp.tile(k_norm_g, (1, n_q))
    cg, sg = _dft_tables(HEAD_DIM)
    n_fg = d_four // HEAD_DIM
    eye_g = jnp.eye(n_fg, dtype=F32)
    chan_cs = jnp.concatenate([jnp.kron(eye_g, cg), jnp.kron(eye_g, sg)], 1)
    ct_c, st_c = _dft_tables(n_ctx)
    ct_l, st_l = _dft_tables(n_lat)
    tables_ctx = (ct_c.astype(BF16), (-st_c).astype(BF16))
    tables_lat = (ct_l.astype(BF16), (-st_l).astype(BF16))
    w_in_r = jnp.concatenate([w_in[:, :, d_att:d_att + 4 * d_rwkv], w_in[:, :, :d_att],
                              w_in[:, :, d_att + 4 * d_rwkv:]], -1).astype(BF16)
    att_col = 4 * d_rwkv
    pool_col = att_col + d_att
    wgu = w_ffn_in.astype(BF16)
    wdn = w_ffn_out.astype(BF16)
    w_out_b = w_out.astype(BF16)

    for l in range(depth):
        last = l == depth - 1
        mod_l = mod[l].reshape(8, 9, d)
        xs, z = _ffn_in(xs, mod_l, wgu[l, 0], wdn[l, 0], ln_g[l, 0], ln_b[l, 0], w_in_r[l],
                        alpha=alpha, n_batch=n_batch)
        att = _attention(z, cos_t, sin_t, qg[l:l + 1], kg[l:l + 1], n_ctx=n_ctx, att_col=att_col)
        p = dict(rwkv_mu=rwkv_mu[l], decay_w0=decay_w0[l], decay_w1=decay_w1[l], decay_w2=decay_w2[l],
                 icl_a0=icl_a0[l], icl_a1=icl_a1[l], icl_a2=icl_a2[l], gate_g1=gate_g1[l], gate_g2=gate_g2[l],
                 k_k=k_k[l], k_a=k_a[l], r_k=r_k[l])
        rwkv_parts = _rwkv(z, p, n_ctx=n_ctx)
        pool, ab = _pool_chan_dft(z, _block_diag(pool_w[l]), pool_scale[l], chan_cs, n_ctx=n_ctx, col=pool_col)
        four_c, four_l = _fourier(ab, tables_ctx, tables_lat, fourier_w[l], n_ctx=n_ctx)
        xs = _out_ffn(xs, att, rwkv_parts, pool, four_c, four_l, mod_l, gn_g[l], gn_b[l], w_out_b[l],
                      ln_g[l, 1], ln_b[l, 1], wgu[l, 1], wdn[l, 1], ln_g[l, 2], ln_b[l, 2],
                      alpha=alpha, n_batch=n_batch, latent_only=last)
    return xs
```

```python
import functools
import math

import jax
import jax.numpy as jnp
from jax import lax
from jax.experimental import pallas as pl
from jax.experimental.pallas import tpu as pltpu

F32 = jnp.float32
BF16 = jnp.bfloat16

HEAD_DIM = 64
GRID_W = 64
POOL_WINDOWS = (2, 4, 8, 16)
ROPE_THETA = 10000.0
LN_EPS = 1e-5
QK_EPS = 1e-6
GN_EPS = 64e-5
N_KV_HEADS = 2

TOKEN_TILE = 256
CHUNK = 64
FFN_COL_BLOCK = 256
ATTN_ROW_SPLIT = 4
INVERSE_HI_LEVELS = 2
HALO = 8
VMEM_LIMIT = 56 * 1024 * 1024


def _cparams(sem, **kw):
    return pltpu.CompilerParams(dimension_semantics=sem, vmem_limit_bytes=VMEM_LIMIT, **kw)


def _dot(a, b):
    return jnp.dot(a, b, preferred_element_type=F32)


def _dot_nt(a, b):
    return lax.dot_general(a, b, (((1,), (1,)), ((), ())), preferred_element_type=F32)


def _dot_tn(a, b):
    return lax.dot_general(a, b, (((0,), (0,)), ((), ())), preferred_element_type=F32)


def _hi_lo(x):
    hi = x.astype(BF16)
    lo = (x - hi.astype(F32)).astype(BF16)
    return hi, lo


def _mm1(a, b, dot=_dot):
    return dot(a.astype(BF16), b.astype(BF16))


def _mm3(a, b):
    m = a.shape[0]
    a1, a2 = _hi_lo(a)
    b1, b2 = _hi_lo(b)
    top = _dot(jnp.concatenate([a1, a2], 0), b1)
    return top[:m] + (top[m:] + _dot(a1, b2))


def _mm_sel(sel_bf16, x):
    x1, x2 = _hi_lo(x)
    return _dot(sel_bf16, x1) + _dot(sel_bf16, x2)


def _mm_sel_r(x, sel_bf16):
    x1, x2 = _hi_lo(x)
    return _dot(x1, sel_bf16) + _dot(x2, sel_bf16)


def _group_ones(n, group):
    r = lax.broadcasted_iota(jnp.int32, (n, n), 0) // group
    c = lax.broadcasted_iota(jnp.int32, (n, n), 1) // group
    return r == c


def _layer_norm(t, g, b):
    mu = jnp.mean(t, -1, keepdims=True)
    tc = t - mu
    var = jnp.mean(tc * tc, -1, keepdims=True)
    return tc * lax.rsqrt(var + LN_EPS) * g + b


def _mod_kernel(c_ref, w_ref, b_ref, o_ref):
    c = c_ref[...]
    s = c * jax.nn.sigmoid(c)
    o_ref[0] = _mm1(s, w_ref[0]) + b_ref[0]


def _modulation(cc, w_mod, b_mod):
    depth, d, n = w_mod.shape
    tn = n // 4
    return pl.pallas_call(
        _mod_kernel,
        grid=(depth, n // tn),
        in_specs=[
            pl.BlockSpec((8, d), lambda l, j: (0, 0)),
            pl.BlockSpec((1, d, tn), lambda l, j: (l, 0, j)),
            pl.BlockSpec((1, 1, tn), lambda l, j: (l, 0, j)),
        ],
        out_specs=pl.BlockSpec((1, 8, tn), lambda l, j: (l, 0, j)),
        out_shape=jax.ShapeDtypeStruct((depth, 8, n), F32),
        compiler_params=_cparams(("arbitrary", "arbitrary")),
        name="modulation",
    )(cc, w_mod, b_mod.reshape(depth, 1, n))


def _mod_row_map(n_batch):
    return lambda b, i: (jnp.where(i == 0, n_batch, b), 0, 0)


def _mod_rows(m_ref, sub):
    return tuple(m_ref[0, 3 * sub + j:3 * sub + j + 1, :] for j in range(3))


def _swiglu_postnorm(x, mods, wgu_ref, wdn_ref, g, b, alpha):
    shift, scale, gate = mods
    dff = wdn_ref.shape[0]
    h = (x * (1.0 + scale) + shift).astype(BF16)
    acc = jnp.zeros(x.shape, F32)
    n = dff // FFN_COL_BLOCK
    gu, act = [None] * n, [None] * n
    for t in range(n + 2):
        if t < n:
            lo, hi = t * FFN_COL_BLOCK, (t + 1) * FFN_COL_BLOCK
            gu[t] = (_dot(h, wgu_ref[:, lo:hi]), _dot(h, wgu_ref[:, dff + lo:dff + hi]))
        if 0 <= t - 1 < n:
            gg, uu = gu[t - 1]
            act[t - 1] = (gg * jax.nn.sigmoid(gg) * uu).astype(BF16)
        if 0 <= t - 2 < n:
            lo, hi = (t - 2) * FFN_COL_BLOCK, (t - 1) * FFN_COL_BLOCK
            acc = acc + _dot(act[t - 2], wdn_ref[lo:hi, :])
    return _layer_norm(alpha * x + (0.5 * gate) * acc, g, b)


def _ffn_in_kernel(*refs, alpha, split_input):
    if split_input:
        (xc_ref, xl_ref, m_ref, wgu_ref, wdn_ref, g_ref, b_ref, win_ref, cos_ref, sin_ref, qg_ref, kg_ref,
         o_ref, z_ref, q_ref, kt_ref, v_ref) = refs
        x = jnp.where(pl.program_id(1) == 0, xc_ref[0], xl_ref[0])
    else:
        (x_ref, m_ref, wgu_ref, wdn_ref, g_ref, b_ref, win_ref, cos_ref, sin_ref, qg_ref, kg_ref,
         o_ref, z_ref, q_ref, kt_ref, v_ref) = refs
        x = x_ref[0]
    xo = _swiglu_postnorm(x, _mod_rows(m_ref, 0), wgu_ref, wdn_ref, g_ref[...], b_ref[...], alpha)
    o_ref[0] = xo
    shift, scale, _ = _mod_rows(m_ref, 1)
    z = _dot((xo * (1.0 + scale) + shift).astype(BF16), win_ref[...])
    nz = z_ref.shape[2]
    z_ref[0] = z[:, :nz]
    _qkv_prep(z[:, nz:], cos_ref[...], sin_ref[...], qg_ref[...], kg_ref[...], q_ref, kt_ref, v_ref)


def _resident(shape):
    return pl.BlockSpec(shape, lambda bb, i: (0,) * len(shape), pipeline_mode=pl.Buffered(1))


def _ffn_in(xs, mod_l, wgu, wdn, ln_g, ln_b, w_in, cos_t, sin_t, qg, kg, *, alpha, n_batch):
    split_input = isinstance(xs, tuple)
    tm = TOKEN_TILE
    if split_input:
        b, t, d = xs[1].shape[0], xs[0].shape[1] + xs[1].shape[1], xs[1].shape[2]
        x_specs = [pl.BlockSpec((1, tm, d), lambda bb, i: (bb, 0, 0)),
                   pl.BlockSpec((1, tm, d), lambda bb, i: (bb, jnp.maximum(i - 1, 0), 0))]
    else:
        b, t, d = xs.shape
        xs = (xs,)
        x_specs = [pl.BlockSpec((1, tm, d), lambda bb, i: (bb, i, 0))]
    dff = wdn.shape[0]
    n = w_in.shape[1]
    dq = qg.shape[1]
    n_q = dq // HEAD_DIM
    dk = N_KV_HEADS * HEAD_DIM
    nz = n - dq - 2 * dk
    if n_q != 2 * N_KV_HEADS or N_KV_HEADS != 2:
        raise ValueError("attention kernel is written for 2 kv heads x 2 query heads")
    if dff % FFN_COL_BLOCK:
        raise ValueError("FFN width must be a multiple of FFN_COL_BLOCK")
    return pl.pallas_call(
        functools.partial(_ffn_in_kernel, alpha=alpha, split_input=split_input),
        grid=(b, t // tm),
        in_specs=x_specs + [
            pl.BlockSpec((1, 9, d), _mod_row_map(n_batch)),
            _resident((d, 2 * dff)), _resident((dff, d)), _resident((1, d)), _resident((1, d)),
            _resident((d, n)),
            pl.BlockSpec((tm, dq), lambda bb, i: (i, 0)), pl.BlockSpec((tm, dq), lambda bb, i: (i, 0)),
            _resident((1, dq)), _resident((1, dk)),
        ],
        out_specs=[pl.BlockSpec((1, tm, d), lambda bb, i: (bb, i, 0)),
                   pl.BlockSpec((1, tm, nz), lambda bb, i: (bb, i, 0)),
                   pl.BlockSpec((1, n_q, tm, HEAD_DIM), lambda bb, i: (bb, 0, i, 0)),
                   pl.BlockSpec((1, dk, tm), lambda bb, i: (bb, 0, i)),
                   pl.BlockSpec((1, N_KV_HEADS, tm, dk), lambda bb, i: (bb, 0, i, 0))],
        out_shape=[jax.ShapeDtypeStruct((b, t, d), F32), jax.ShapeDtypeStruct((b, t, nz), F32),
                   jax.ShapeDtypeStruct((b, n_q, t, HEAD_DIM), BF16),
                   jax.ShapeDtypeStruct((b, dk, t), BF16),
                   jax.ShapeDtypeStruct((b, N_KV_HEADS, t, dk), BF16)],
        compiler_params=_cparams(("arbitrary", "arbitrary")),
        name="ffn_in",
    )(*xs, mod_l, wgu, wdn, ln_g.reshape(1, d), ln_b.reshape(1, d), w_in, cos_t, sin_t, qg, kg[:, :dk])


def _pair_swap(x):
    n = x.shape[-1]
    lane = lax.broadcasted_iota(jnp.int32, x.shape, 1)
    nxt = pltpu.roll(x, n - 1, 1)
    prv = pltpu.roll(x, 1, 1)
    return jnp.where(lane % 2 == 0, nxt, prv)


def _qkv_prep(z, cos, sin, qg, kg, q_ref, kt_ref, v_ref):
    n_q, n_kv = q_ref.shape[1], v_ref.shape[1]
    dq, dk = n_q * HEAD_DIM, n_kv * HEAD_DIM
    zq, zk, zv = z[:, :dq], z[:, dq:dq + dk], z[:, dq + dk:dq + 2 * dk]
    xq = zq * qg
    yq = xq * cos + _pair_swap(xq) * sin
    xk = zk * kg
    yk = xk * cos[:, :dk] + _pair_swap(xk) * sin[:, :dk]
    q_scale = HEAD_DIM ** -0.5 * math.log2(math.e)
    for h in range(n_q):
        sl = slice(h * HEAD_DIM, (h + 1) * HEAD_DIM)
        ms = jnp.mean(zq[:, sl] * zq[:, sl], -1, keepdims=True)
        q_ref[0, h] = (yq[:, sl] * (lax.rsqrt(ms + QK_EPS) * q_scale)).astype(BF16)
    lane = lax.broadcasted_iota(jnp.int32, (1, dk), 1)
    inv = jnp.zeros(zk.shape, F32)
    for h in range(n_kv):
        sl = slice(h * HEAD_DIM, (h + 1) * HEAD_DIM)
        in_head = (lane // HEAD_DIM) == h
        ms = jnp.mean(zk[:, sl] * zk[:, sl], -1, keepdims=True)
        inv = jnp.where(in_head, lax.rsqrt(ms + QK_EPS), inv)
        v_ref[0, h] = jnp.where(in_head, zv, 1.0).astype(BF16)
    kt_ref[0] = (yk * inv).T.astype(BF16)


def _attn_kernel(q_ref, kt_ref, v_ref, o_ref, *, n_ctx, group):
    hkv = pl.program_id(1)
    i = pl.program_id(2)
    tq = q_ref.shape[2]
    q = q_ref[0].reshape(group * tq, HEAD_DIM)

    def run(kt, vv):
        n = ATTN_ROW_SPLIT
        rb = group * tq // n
        s, p, parts = [None] * n, [None] * n, [None] * n
        for t in range(n + 2):
            if t < n:
                s[t] = _dot(q[t * rb:(t + 1) * rb], kt)
            if 0 <= t - 1 < n:
                p[t - 1] = jnp.exp2(s[t - 1] - jnp.max(s[t - 1], -1, keepdims=True)).astype(BF16)
            if 0 <= t - 2 < n:
                parts[t - 2] = _dot(p[t - 2], vv)
        o = jnp.concatenate(parts, 0)
        r = o / pltpu.roll(o, HEAD_DIM, 1)
        rs = pltpu.roll(r, HEAD_DIM, 1)
        lane = lax.broadcasted_iota(jnp.int32, (1, 2 * HEAD_DIM), 1)
        first = hkv % 2 == 0
        low = jnp.where(first, r[:tq], rs[:tq])
        high = jnp.where(first, rs[tq:], r[tq:])
        o_ref[0] = jnp.where(lane < HEAD_DIM, low, high)

    @pl.when(i == 0)
    def _():
        run(kt_ref[0, :, :n_ctx], v_ref[0, 0, :n_ctx])

    @pl.when(i > 0)
    def _():
        run(kt_ref[0], v_ref[0, 0])


def _attention(q, kt, v, *, n_ctx):
    b, n_q, t, _ = q.shape
    n_kv = v.shape[1]
    group = n_q // n_kv
    tm = TOKEN_TILE
    return pl.pallas_call(
        functools.partial(_attn_kernel, n_ctx=n_ctx, group=group),
        grid=(b, n_kv, t // tm),
        in_specs=[
            pl.BlockSpec((1, group, tm, HEAD_DIM), lambda bb, h, i: (bb, h, i, 0)),
            pl.BlockSpec((1, HEAD_DIM, t), lambda bb, h, i: (bb, h, 0)),
            pl.BlockSpec((1, 1, t, n_kv * HEAD_DIM), lambda bb, h, i: (bb, h, 0, 0)),
        ],
        out_specs=pl.BlockSpec((1, tm, group * HEAD_DIM), lambda bb, h, i: (bb, i, h)),
        out_shape=jax.ShapeDtypeStruct((b, t, n_q * HEAD_DIM), F32),
        compiler_params=_cparams(("arbitrary", "arbitrary", "arbitrary")),
        name="attention",
    )(q, kt, v)


def _halo_specs(width, col_block, rows_per_tile, n_tiles, tile_of=lambda i: i):
    r = rows_per_tile // HALO

    def prev_map(bb, i, *_):
        return (bb, jnp.maximum(tile_of(i) * r - 1, 0), col_block)

    def next_map(bb, i, *_):
        return (bb, jnp.minimum((tile_of(i) + 1) * r, n_tiles * r - 1), col_block)

    return (pl.BlockSpec((1, HALO, width), prev_map), pl.BlockSpec((1, HALO, width), next_map))


def _segment_flags(i, n_tiles):
    has_prev = i >= 2
    has_next = jnp.logical_and(i >= 1, i < n_tiles - 1)
    return has_prev, has_next


def _tri_inverse_many(lmats):
    n = lmats[0].shape[0]
    eye = (lax.broadcasted_iota(jnp.int32, (n, n), 0) == lax.broadcasted_iota(jnp.int32, (n, n), 1))
    accs = [jnp.where(eye, 1.0, 0.0) + lm for lm in lmats]
    powers = list(lmats)
    for level in range(int(math.log2(CHUNK)) - 1):
        mm = _mm3 if level < INVERSE_HI_LEVELS else _mm1
        powers = [mm(p, p) for p in powers]
        accs = [a + mm(p, a) for p, a in zip(powers, accs)]
    return accs


def _icl_rate(xa, d, a0_ref, a1_ref, a2_ref):
    return jax.nn.sigmoid(a0_ref[d:d + 1, :] + _mm1(_mm1(xa, a1_ref[d]), a2_ref[d]))


def _chunk_setup(d, r, v, kk, kd, aicl, logw):
    tm, dr = r.shape
    ri = lax.broadcasted_iota(jnp.int32, (tm, tm), 0)
    ci = lax.broadcasted_iota(jnp.int32, (tm, tm), 1)
    same_chunk = (ri // CHUNK) == (ci // CHUNK)
    before = (ci < ri) if d == 0 else (ci > ri)
    strict = jnp.logical_and(same_chunk, before)
    incl = jnp.logical_and(same_chunk, jnp.logical_or(before, ci == ri))
    cw = _mm_sel(jnp.where(incl, 1.0, 0.0).astype(BF16), logw)
    tot = _mm_sel(jnp.where(same_chunk, 1.0, 0.0).astype(BF16), logw)
    e_neg = jnp.exp(-cw)
    e_rem = jnp.exp(tot - cw)
    bvec = kk * aicl
    a_tb = (-kk * jnp.exp(cw - logw)).astype(BF16)
    r_t = r * jnp.exp(cw)
    r_tb = r_t.astype(BF16)
    v_b = v.astype(BF16)
    rhs_nt = jnp.concatenate([bvec * e_neg, kd * e_neg], 0).astype(BF16)
    lane = lax.broadcasted_iota(jnp.int32, (1, dr), 1)
    zero_b = jnp.zeros((), BF16)

    head_masks = [(lane // HEAD_DIM) == h for h in range(dr // HEAD_DIM)]
    a_ms = [jnp.where(mh, a_tb, zero_b) for mh in head_masks]
    lhs = jnp.concatenate([x for h, mh in enumerate(head_masks) for x in (a_ms[h], jnp.where(mh, r_tb, zero_b))], 0)
    prod_all = _dot_nt(lhs, rhs_nt)
    incl2 = jnp.concatenate([incl, incl], 1)
    labs, mrs, xvs = [], [], []
    for h, mh in enumerate(head_masks):
        prod = prod_all[h * 2 * tm:(h + 1) * 2 * tm]
        labs.append(jnp.where(strict, prod[:tm, :tm], 0.0))
        lak = jnp.where(strict, prod[:tm, tm:], 0.0).astype(BF16)
        mr = jnp.where(incl2, prod[tm:, :], 0.0).astype(BF16)
        mrs.append(mr)
        xvs.append(_dot(jnp.concatenate([lak, mr[:, tm:]], 0), jnp.where(mh, v_b, zero_b)))
    return dict(d=d, labs=labs, mrs=mrs, xvs=xvs, a_ms=a_ms, r_t=r_t, v_b=v_b, tot=tot,
                bb_b=(bvec * e_rem).astype(BF16), kb_b=(kd * e_rem).astype(BF16))


def _chunk_finish(setups, tinvs, states):
    tm, dr = setups[0]["r_t"].shape
    n_heads = dr // HEAD_DIM
    n_chunks = tm // CHUNK
    pqs = [[_dot(tinvs[j][h].astype(BF16), jnp.concatenate([s["a_ms"][h], s["xvs"][h][:tm].astype(BF16)], 1))
            for h in range(n_heads)] for j, s in enumerate(setups)]
    rys, y0s, p_bs, q_bs = [], [], [], []
    for j, s in enumerate(setups):
        pq_nat = jnp.zeros((tm, 2 * dr), F32)
        ry = s["r_t"]
        y0 = jnp.zeros((tm, dr), F32)
        for h in range(n_heads):
            pq_nat = pq_nat + pqs[j][h]
            upd = _dot(s["mrs"][h][:, :tm], pqs[j][h].astype(BF16))
            ry = ry + upd[:, :dr]
            y0 = y0 + (upd[:, dr:] + s["xvs"][h][tm:])
        rys.append(ry)
        y0s.append(y0)
        p_bs.append(pq_nat[:, :dr].astype(BF16))
        q_bs.append(pq_nat[:, dr:].astype(BF16))

    same_head = _group_ones(dr, HEAD_DIM)
    eye = lax.broadcasted_iota(jnp.int32, (dr, dr), 0) == lax.broadcasted_iota(jnp.int32, (dr, dr), 1)
    states = list(states)
    ys = [[None] * n_chunks for _ in setups]
    for step in range(n_chunks):
        for j, s in enumerate(setups):
            c = step if s["d"] == 0 else n_chunks - 1 - step
            sl = slice(c * CHUNK, (c + 1) * CHUNK)
            gmat = _dot_tn(s["bb_b"][sl], p_bs[j][sl])
            hmat = _dot_tn(jnp.concatenate([s["bb_b"][sl], s["kb_b"][sl]], 0),
                           jnp.concatenate([q_bs[j][sl], s["v_b"][sl]], 0))
            decay = jnp.exp(s["tot"][c * CHUNK:c * CHUNK + 1, :])
            gmat = jnp.where(same_head, gmat, 0.0) + jnp.where(eye, decay, 0.0)
            ys[j][c] = _mm1(rys[j][sl], states[j]) + y0s[j][sl]
            states[j] = _mm3(gmat, states[j]) + jnp.where(same_head, hmat, 0.0)
    return [jnp.concatenate(y, 0) for y in ys], states


def _rwkv_scan_kernel(zf_ref, zfp_ref, zfn_ref, zr_ref, zrp_ref, zrn_ref, mu_ref, kk_ref, ka_ref, rk_ref,
                      w0_ref, w1_ref, w2_ref, a0_ref, a1_ref, a2_ref, g1_ref, g2_ref,
                      yf_ref, yr_ref, bonus_ref, gate_ref, st_ref, *, dr):
    i = pl.program_id(1)
    n_tiles = pl.num_programs(1)
    tm = zf_ref.shape[1]

    @pl.when(i == 0)
    def _():
        st_ref[...] = jnp.zeros(st_ref.shape, F32)

    head_ones = jnp.where(_group_ones(dr, HEAD_DIM), 1.0, 0.0).astype(BF16)
    mu = mu_ref[...]
    row = lax.broadcasted_iota(jnp.int32, (tm, 1), 0)
    tile_rev = jnp.where(i == 0, 0, n_tiles - i)
    setups = []
    for d, (z_ref, zp_ref, zn_ref, tile) in enumerate(((zf_ref, zfp_ref, zfn_ref, i), (zr_ref, zrp_ref, zrn_ref, tile_rev))):
        z = z_ref[0]
        has_prev, has_next = _segment_flags(tile, n_tiles)
        prow = jnp.where(has_prev, zp_ref[0, HALO - 1:HALO, :], 0.0)
        nrow = jnp.where(has_next, zn_ref[0, 0:1, :], 0.0)
        zprev = jnp.where(row == 0, prow, pltpu.roll(z, 1, 0))
        znext = jnp.where(row == tm - 1, nrow, pltpu.roll(z, tm - 1, 0))
        dz = 0.5 * (zprev + znext) - z
        r = z[:, 0:dr] + dz[:, 0:dr] * mu[0:1]
        k = z[:, dr:2 * dr] + dz[:, dr:2 * dr] * mu[1:2]
        v = z[:, 2 * dr:3 * dr] + dz[:, 2 * dr:3 * dr] * mu[2:3]
        zu, du = z[:, 3 * dr:4 * dr], dz[:, 3 * dr:4 * dr]
        xw = zu + du * mu[3:4]
        xa = zu + du * mu[4:5]
        kk = k * kk_ref[...]
        kk = kk / jnp.maximum(jnp.sqrt(_mm_sel_r(kk * kk, head_ones)), 1e-12)
        w_raw = w0_ref[d:d + 1, :] + _mm1(jnp.tanh(_mm1(xw, w1_ref[d])), w2_ref[d])
        logw = (-math.exp(-0.5)) * jax.nn.sigmoid(w_raw)
        aicl = _icl_rate(xa, d, a0_ref, a1_ref, a2_ref)
        kd = k * (1.0 + (aicl - 1.0) * ka_ref[...])
        if d == 0:
            xg = zu + du * mu[5:6]
            gate_ref[0] = _mm1(jax.nn.sigmoid(_mm1(xg, g1_ref[...])), g2_ref[...])
            kd_other = k * (1.0 + (_icl_rate(xa, 1, a0_ref, a1_ref, a2_ref) - 1.0) * ka_ref[...])
            bonus_ref[0] = _mm_sel_r(r * (kd + kd_other) * rk_ref[...], head_ones) * v
        setups.append(_chunk_setup(d, r, v, kk, kd, aicl, logw))
    n_heads = dr // HEAD_DIM
    flat = _tri_inverse_many([lab for s in setups for lab in s["labs"]])
    tinvs = [flat[j * n_heads:(j + 1) * n_heads] for j in range(len(setups))]
    ys, states = _chunk_finish(setups, tinvs, [st_ref[0], st_ref[1]])
    yf_ref[0] = ys[0]
    yr_ref[0] = ys[1]
    st_ref[0] = states[0]
    st_ref[1] = states[1]


def _rwkv(z, p, *, n_ctx):
    b, t, _ = z.shape
    dr = p["k_k"].shape[-1]
    tm = TOKEN_TILE
    nt = t // tm
    wz = 4 * dr
    if n_ctx != tm:
        raise ValueError("the context segment must be exactly one token tile")

    def tile_rev(i):
        return jnp.where(i == 0, 0, nt - i)

    fprev, fnext = _halo_specs(wz, 0, tm, nt)
    rprev, rnext = _halo_specs(wz, 0, tm, nt, tile_rev)

    def full(shape):
        return pl.BlockSpec(shape, lambda bb, i: (0,) * len(shape))

    out = jax.ShapeDtypeStruct((b, t, dr), F32)
    return pl.pallas_call(
        functools.partial(_rwkv_scan_kernel, dr=dr),
        grid=(b, nt),
        in_specs=[
            pl.BlockSpec((1, tm, wz), lambda bb, i: (bb, i, 0)), fprev, fnext,
            pl.BlockSpec((1, tm, wz), lambda bb, i: (bb, tile_rev(i), 0)), rprev, rnext,
            full((6, dr)), full((1, dr)), full((1, dr)), full((1, dr)),
            full((2, dr)), full(p["decay_w1"].shape), full(p["decay_w2"].shape),
            full((2, dr)), full(p["icl_a1"].shape), full(p["icl_a2"].shape),
            full(p["gate_g1"].shape), full(p["gate_g2"].shape),
        ],
        out_specs=[
            pl.BlockSpec((1, tm, dr), lambda bb, i: (bb, i, 0)),
            pl.BlockSpec((1, tm, dr), lambda bb, i: (bb, tile_rev(i), 0)),
            pl.BlockSpec((1, tm, dr), lambda bb, i: (bb, i, 0)),
            pl.BlockSpec((1, tm, dr), lambda bb, i: (bb, i, 0)),
        ],
        out_shape=[out, out, out, out],
        scratch_shapes=[pltpu.VMEM((2, dr, dr), F32)],
        compiler_params=_cparams(("arbitrary", "arbitrary")),
        name="rwkv_scan",
    )(z, z, z, z, z, z, p["rwkv_mu"], p["k_k"].reshape(1, dr), p["k_a"].reshape(1, dr), p["r_k"].reshape(1, dr),
      p["decay_w0"], p["decay_w1"], p["decay_w2"], p["icl_a0"], p["icl_a1"], p["icl_a2"],
      p["gate_g1"], p["gate_g2"])


def _pool_kernel(z_ref, zp_ref, zn_ref, w_ref, sc_ref, cs_ref, o_ref, ab_ref, *, n_ctx):
    i = pl.program_id(1)
    n_tiles = pl.num_programs(1)
    tm, dp = z_ref.shape[1], w_ref.shape[0]
    has_prev, has_next = _segment_flags(i, n_tiles)
    x = z_ref[0, :, :dp]
    ab_ref[0] = _mm3(z_ref[0, :, dp:], cs_ref[...]).astype(BF16)
    xe = jnp.concatenate([jnp.where(has_prev, zp_ref[0, :, :dp], 0.0), x,
                          jnp.where(has_next, zn_ref[0, :, :dp], 0.0)], 0)
    ne = tm + 2 * HALO

    def back(a, s):
        return pltpu.roll(a, s, 0)

    def ahead(a, s):
        return pltpu.roll(a, ne - s, 0)

    d2 = xe + back(xe, 1)
    d4 = d2 + back(d2, 2)
    d8 = d4 + back(d4, 4)
    e2 = xe + ahead(xe, 1)
    e4 = e2 + ahead(e2, 2)
    f1 = ahead(xe, 1)
    f3 = f1 + ahead(e2, 2)
    f7 = f3 + ahead(e4, 4)
    sums = (d2, d2 + back(xe, 2) + f1, d4 + back(xe, 4) + f3, d8 + back(xe, 8) + f7)

    tglob = i * tm + lax.broadcasted_iota(jnp.int32, (tm, 1), 0)
    n_lat = n_tiles * tm - n_ctx
    pos = jnp.where(i == 0, tglob, tglob - n_ctx)
    seg = jnp.where(i == 0, n_ctx, n_lat)
    lane = lax.broadcasted_iota(jnp.int32, (1, dp), 1)
    gw = dp // len(POOL_WINDOWS)
    pooled = jnp.zeros((tm, dp), F32)
    for g, win in enumerate(POOL_WINDOWS):
        nb, nf = win // 2, win - win // 2 - 1
        cnt = (jnp.minimum(pos, nb) + 1 + jnp.minimum(seg - 1 - pos, nf)).astype(F32)
        mean = sums[g][HALO:HALO + tm] / cnt
        pooled = jnp.where((lane // gw) == g, mean - x, pooled)
    o_ref[0] = _mm1(pooled, w_ref[...]) * sc_ref[...]


def _pool_chan_dft(z, w_bd, scale, chan_cs, *, n_ctx, col):
    b, t, _ = z.shape
    dp = w_bd.shape[0]
    df = chan_cs.shape[0]
    wz = dp + df
    tm = TOKEN_TILE
    if col % wz:
        raise ValueError("pool / Fourier columns must start on a multiple of their width")
    prev_spec, next_spec = _halo_specs(wz, col // wz, tm, t // tm)
    return pl.pallas_call(
        functools.partial(_pool_kernel, n_ctx=n_ctx),
        grid=(b, t // tm),
        in_specs=[
            pl.BlockSpec((1, tm, wz), lambda bb, i: (bb, i, col // wz)), prev_spec, next_spec,
            pl.BlockSpec((dp, dp), lambda bb, i: (0, 0)),
            pl.BlockSpec((1, dp), lambda bb, i: (0, 0)),
            pl.BlockSpec((df, 2 * df), lambda bb, i: (0, 0)),
        ],
        out_specs=[pl.BlockSpec((1, tm, dp), lambda bb, i: (bb, i, 0)),
                   pl.BlockSpec((1, tm, 2 * df), lambda bb, i: (bb, i, 0))],
        out_shape=[jax.ShapeDtypeStruct((b, t, dp), F32), jax.ShapeDtypeStruct((b, t, 2 * df), BF16)],
        compiler_params=_cparams(("arbitrary", "arbitrary")),
        name="pool_chan_dft",
    )(z, z, z, w_bd, scale.reshape(1, dp), chan_cs)


def _time_dft_kernel(ct_ref, st_ref, ab_ref, w_ref, o_ref, *, norm, row0):
    df = w_ref.shape[0]
    rows = ct_ref.shape[1]
    ab = ab_ref[0, row0:row0 + rows, :]
    f = (_dot(ct_ref[...], ab[:, :df]) + _dot(st_ref[...], ab[:, df:])) * norm
    o_ref[0] = _mm1(f, w_ref[...])


def _dft_tables(n, dtype=F32):
    n2 = 1 << (int(math.log2(n)) // 2)
    n1 = n // n2
    if n1 * n2 != n:
        raise ValueError("DFT length must be a power of two")
    tp = jnp.arange(n, dtype=jnp.int32)[:, None]
    alpha = ((tp * jnp.arange(n1, dtype=jnp.int32)[None, :]) % n1).astype(F32) * (2.0 * math.pi / n1)
    beta = ((tp * jnp.arange(n2, dtype=jnp.int32)[None, :]) % n).astype(F32) * (2.0 * math.pi / n)
    ca, sa = jnp.cos(alpha)[:, :, None], jnp.sin(alpha)[:, :, None]
    cb, sb = jnp.cos(beta)[:, None, :], jnp.sin(beta)[:, None, :]
    return (ca * cb - sa * sb).astype(dtype).reshape(n, n), (sa * cb + ca * sb).astype(dtype).reshape(n, n)


def _fourier_segment(ab, tables, w, *, row0, n_chan_group):
    b, t, _ = ab.shape
    df = w.shape[0]
    ct, st = tables
    rows = ct.shape[0]
    tr = min(rows, 512)
    norm = 1.0 / math.sqrt(rows * n_chan_group)
    return pl.pallas_call(
        functools.partial(_time_dft_kernel, norm=norm, row0=row0),
        grid=(rows // tr, b),
        in_specs=[
            pl.BlockSpec((tr, rows), lambda i, bb: (i, 0)),
            pl.BlockSpec((tr, rows), lambda i, bb: (i, 0)),
            pl.BlockSpec((1, t, 2 * df), lambda i, bb: (bb, 0, 0)),
            pl.BlockSpec((df, df), lambda i, bb: (0, 0)),
        ],
        out_specs=pl.BlockSpec((1, tr, df), lambda i, bb: (bb, i, 0)),
        out_shape=jax.ShapeDtypeStruct((b, rows, df), F32),
        compiler_params=_cparams(("arbitrary", "arbitrary")),
        name=f"time_dft_{rows}",
    )(ct, st, ab, w)


def _fourier(ab, tables_ctx, tables_lat, w, *, n_ctx):
    four_c = _fourier_segment(ab, tables_ctx, w, row0=0, n_chan_group=HEAD_DIM)
    four_l = _fourier_segment(ab, tables_lat, w, row0=n_ctx, n_chan_group=HEAD_DIM)
    return four_c, four_l


def _out_ffn_kernel(x_ref, a_ref, yf_ref, yr_ref, bn_ref, gt_ref, p_ref, fc_ref, fl_ref, m_ref, gng_ref, gnb_ref,
                    wout_ref, g1_ref, b1_ref, wgu_ref, wdn_ref, g2_ref, b2_ref, o_ref, *, alpha, skip):
    x = x_ref[0]
    dr = yf_ref.shape[2]
    y = yf_ref[0] + yr_ref[0]
    head_ones = jnp.where(_group_ones(dr, HEAD_DIM), 1.0, 0.0).astype(BF16)
    mu = _mm_sel_r(y, head_ones) * (1.0 / HEAD_DIM)
    yc = y - mu
    var = _mm_sel_r(yc * yc, head_ones) * (1.0 / HEAD_DIM)
    rw = (yc * lax.rsqrt(var + GN_EPS) * gng_ref[...] + gnb_ref[...] + bn_ref[0]) * gt_ref[0]
    four = fl_ref[0] if skip else jnp.where(pl.program_id(1) == 0, fc_ref[0], fl_ref[0])
    gate = _mod_rows(m_ref, 1)[2]
    cat = jnp.concatenate([a_ref[0], rw, p_ref[0], four], -1).astype(BF16)
    x1 = _layer_norm(alpha * x + gate * _dot(cat, wout_ref[...]), g1_ref[...], b1_ref[...])
    o_ref[0] = _swiglu_postnorm(x1, _mod_rows(m_ref, 2), wgu_ref, wdn_ref, g2_ref[...], b2_ref[...], alpha)


def _out_ffn(xs, att, rwkv_parts, pool, four_c, four_l, mod_l, gn_g, gn_b, w_out, ln_g1, ln_b1, wgu, wdn,
             ln_g2, ln_b2, *, alpha, n_batch, latent_only):
    b, t, d = xs.shape
    tm = TOKEN_TILE
    dm = att.shape[-1]
    dff = wdn.shape[0]
    skip = 1 if latent_only else 0
    rows = lambda bb, i: (bb, i + skip, 0)
    mod_map = (lambda bb, i: (bb, 0, 0)) if latent_only else _mod_row_map(n_batch)
    branch = pl.BlockSpec((1, tm, dm), rows)
    return pl.pallas_call(
        functools.partial(_out_ffn_kernel, alpha=alpha, skip=skip),
        grid=(b, t // tm - skip),
        in_specs=[
            pl.BlockSpec((1, tm, d), rows), branch, branch, branch, branch, branch, branch,
            pl.BlockSpec((1, tm, dm), lambda bb, i: (bb, 0, 0)),
            pl.BlockSpec((1, tm, dm), lambda bb, i: (bb, jnp.maximum(i + skip - 1, 0), 0)),
            pl.BlockSpec((1, 9, d), mod_map),
            _resident((1, dm)), _resident((1, dm)),
            _resident((4 * dm, d)), _resident((1, d)), _resident((1, d)),
            _resident((d, 2 * dff)), _resident((dff, d)), _resident((1, d)), _resident((1, d)),
        ],
        out_specs=pl.BlockSpec((1, tm, d), lambda bb, i: (bb, i, 0)),
        out_shape=jax.ShapeDtypeStruct((b, t - skip * tm, d), F32),
        compiler_params=_cparams(("arbitrary", "arbitrary")),
        name="out_ffn",
    )(xs, att, *rwkv_parts, pool, four_c, four_l, mod_l, gn_g.reshape(1, dm), gn_b.reshape(1, dm), w_out,
      ln_g1.reshape(1, d), ln_b1.reshape(1, d), wgu, wdn, ln_g2.reshape(1, d), ln_b2.reshape(1, d))


def _rope_tables(n_ctx, n_lat, n_heads):
    pos = jnp.arange(n_lat)
    rows = (pos // GRID_W).astype(F32)
    cols = (pos % GRID_W).astype(F32)
    n_pair_axis = HEAD_DIM // 4
    inv = ROPE_THETA ** (-jnp.arange(n_pair_axis, dtype=F32) / n_pair_axis)
    ang = jnp.concatenate([rows[:, None] * inv, cols[:, None] * inv], -1)
    cos = jnp.repeat(jnp.cos(ang), 2, axis=-1)
    sin = jnp.stack([-jnp.sin(ang), jnp.sin(ang)], -1).reshape(n_lat, HEAD_DIM)
    cos = jnp.concatenate([jnp.ones((n_ctx, HEAD_DIM), F32), cos], 0)
    sin = jnp.concatenate([jnp.zeros((n_ctx, HEAD_DIM), F32), sin], 0)
    return jnp.tile(cos, (1, n_heads)), jnp.tile(sin, (1, n_heads))


def _block_diag(blocks):
    n, r, c = blocks.shape
    eye = jnp.eye(n, dtype=blocks.dtype)
    return (eye[:, None, :, None] * blocks[:, :, None, :]).reshape(n * r, n * c)


def kernel(x, c, ctx, c_ctx, w_mod, b_mod, ln_g, ln_b, w_ffn_in, w_ffn_out, w_in, q_norm_g, k_norm_g, rwkv_mu, decay_w0, decay_w1, decay_w2, icl_a0, icl_a1, icl_a2, gate_g1, gate_g2, k_k, k_a, r_k, gn_g, gn_b, pool_w, pool_scale, fourier_w, w_out):
    n_batch, n_lat, d = x.shape
    n_ctx = ctx.shape[1]
    depth = w_mod.shape[0]
    if n_ctx != TOKEN_TILE or n_lat % TOKEN_TILE or n_batch + 1 > 8:
        raise ValueError("unsupported shapes")
    alpha = (2 * depth) ** 0.25
    d_rwkv = k_k.shape[-1]
    d_pool = pool_scale.shape[-1]
    d_four = fourier_w.shape[-1]
    d_in = w_in.shape[-1]
    d_att = d_in - 4 * d_rwkv - d_pool - d_four
    n_q = (d_att - 2 * N_KV_HEADS * HEAD_DIM) // HEAD_DIM

    xs = (ctx, x)
    cc = jnp.zeros((8, d), F32).at[:n_batch].set(c).at[n_batch].set(c_ctx)
    mod = _modulation(cc, w_mod, b_mod)

    cos_t, sin_t = _rope_tables(n_ctx, n_lat, n_q)
    qg = jnp.tile(q_norm_g, (1, n_q))
    kg = jnp.tile(k_norm_g, (1, n_q))
    cg, sg = _dft_tables(HEAD_DIM)
    n_fg = d_four // HEAD_DIM
    eye_g = jnp.eye(n_fg, dtype=F32)
    chan_cs = jnp.concatenate([jnp.kron(eye_g, cg), jnp.kron(eye_g, sg)], 1)
    ct_c, st_c = _dft_tables(n_ctx, BF16)
    ct_l, st_l = _dft_tables(n_lat, BF16)
    tables_ctx = (ct_c, -st_c)
    tables_lat = (ct_l, -st_l)
    w_in_r = jnp.concatenate([w_in[:, :, d_att:], w_in[:, :, :d_att]], -1).astype(BF16)
    pool_col = 4 * d_rwkv
    wgu = w_ffn_in.astype(BF16)
    wdn = w_ffn_out.astype(BF16)
    w_out_b = w_out.astype(BF16)

    for l in range(depth):
        last = l == depth - 1
        mod_l = mod[l].reshape(8, 9, d)
        xs, z, q, kt, v = _ffn_in(xs, mod_l, wgu[l, 0], wdn[l, 0], ln_g[l, 0], ln_b[l, 0], w_in_r[l],
                                  cos_t, sin_t, qg[l:l + 1], kg[l:l + 1], alpha=alpha, n_batch=n_batch)
        att = _attention(q, kt, v, n_ctx=n_ctx)
        p = dict(rwkv_mu=rwkv_mu[l], decay_w0=decay_w0[l], decay_w1=decay_w1[l], decay_w2=decay_w2[l],
                 icl_a0=icl_a0[l], icl_a1=icl_a1[l], icl_a2=icl_a2[l], gate_g1=gate_g1[l], gate_g2=gate_g2[l],
                 k_k=k_k[l], k_a=k_a[l], r_k=r_k[l])
        rwkv_parts = _rwkv(z, p, n_ctx=n_ctx)
        pool, ab = _pool_chan_dft(z, _block_diag(pool_w[l]), pool_scale[l], chan_cs, n_ctx=n_ctx, col=pool_col)
        four_c, four_l = _fourier(ab, tables_ctx, tables_lat, fourier_w[l], n_ctx=n_ctx)
        xs = _out_ffn(xs, att, rwkv_parts, pool, four_c, four_l, mod_l, gn_g[l], gn_b[l], w_out_b[l],
                      ln_g[l, 1], ln_b[l, 1], wgu[l, 1], wdn[l, 1], ln_g[l, 2], ln_b[l, 2],
                      alpha=alpha, n_batch=n_batch, latent_only=last)
    return xs
```
